```python
import jax, jax.numpy as jnp
from jax import lax
import numpy as np

D_MODEL = 2048
BATCH = 4
SEQ = 8192
DEPTH = 4

HEAD_DIM = 64
D_MIX = D_MODEL
D_ATTN = D_MIX // 2
D_GMLP = D_MIX - D_ATTN
N_Q_HEADS = D_ATTN // HEAD_DIM
N_KV_HEADS = 4
N_GMLP_HEADS = D_GMLP // HEAD_DIM
WINDOW = 128
CHUNK = 128
RMS_EPS = 1e-6
D_KV = N_KV_HEADS * HEAD_DIM
D_IN = D_ATTN + 2 * D_KV + D_ATTN + 3 * D_GMLP

kernel_name = "hybrid_swa_sink_gmlp_parallel_heads"


def _rmsnorm(x, g):
    xf = x.astype(jnp.float32)
    y = xf * lax.rsqrt(jnp.mean(xf * xf, axis=-1, keepdims=True) + RMS_EPS)
    return (y * g.astype(jnp.float32)).astype(x.dtype)


def _alibi_slopes(n):
    return jnp.asarray(2.0 ** (-8.0 * np.arange(1, n + 1) / n), dtype=jnp.float32)


def _band(t, nb):
    B, S, H, D = t.shape
    tb = t.reshape(B, nb, WINDOW, H, D)
    prev = jnp.pad(tb, ((0, 0), (1, 0), (0, 0), (0, 0), (0, 0)))[:, :-1]
    return jnp.concatenate([prev, tb], axis=2)


def _swa_gqa_sinks(q, k, v, sinks, slopes):
    B, S, Hq, Dh = q.shape
    Hkv = k.shape[2]
    G = Hq // Hkv
    nb = S // WINDOW
    qb = q.reshape(B, nb, WINDOW, Hkv, G, Dh)
    kb = _band(k, nb)
    vb = _band(v, nb)
    scores = jnp.einsum('bnqhgd,bnkhd->bnhgqk', qb, kb).astype(jnp.float32) * (Dh ** -0.5)
    qpos = jnp.arange(WINDOW)[:, None] + WINDOW
    kpos = jnp.arange(2 * WINDOW)[None, :]
    dist = qpos - kpos
    in_window = (dist >= 0) & (dist < WINDOW)
    not_pad = (jnp.arange(nb)[:, None] > 0) | (kpos >= WINDOW)
    mask = in_window[None] & not_pad[:, None, :]
    sl = slopes.reshape(Hkv, G)
    alibi = -sl[:, :, None, None] * dist.astype(jnp.float32)[None, None]
    scores = jnp.where(mask[None, :, None, None], scores + alibi[None, None], -jnp.inf)
    sink = sinks.astype(jnp.float32).reshape(Hkv, G)[None, None, :, :, None, None]
    m = jnp.maximum(jnp.max(scores, axis=-1, keepdims=True), sink)
    p = jnp.exp(scores - m)
    p = p / (jnp.sum(p, axis=-1, keepdims=True) + jnp.exp(sink - m))
    out = jnp.einsum('bnhgqk,bnkhd->bnqhgd', p.astype(v.dtype), vb)
    return out.reshape(B, S, Hq * Dh)


def _chunked_sgu(u, v, w_s, b_s):
    B, S, H, C = v.shape
    nc = S // CHUNK
    vc = v.reshape(B, nc, CHUNK, H, C)
    w = jnp.tril(w_s)
    mixed = jnp.einsum('hts,bnshc->bnthc', w, vc) + jnp.transpose(b_s)[None, None, :, :, None]
    return u * mixed.reshape(B, S, H, C)


def setup_inputs(seed: int = 0) -> dict:
    key = jax.random.key(seed)
    ks = jax.random.split(key, 9)
    f32 = jnp.float32
    x = jax.random.normal(ks[0], (BATCH, SEQ, D_MODEL), f32)
    norm_g = 1.0 + 0.02 * jax.random.normal(ks[1], (DEPTH, D_MODEL), f32)
    w_in = jax.random.normal(ks[2], (DEPTH, D_MODEL, D_IN), f32) * (D_MODEL ** -0.5)
    q_norm = 1.0 + 0.02 * jax.random.normal(ks[3], (DEPTH, HEAD_DIM), f32)
    k_norm = 1.0 + 0.02 * jax.random.normal(ks[4], (DEPTH, HEAD_DIM), f32)
    sinks = 0.5 * jax.random.normal(ks[5], (DEPTH, N_Q_HEADS), f32)
    w_s = jax.random.normal(ks[6], (DEPTH, N_GMLP_HEADS, CHUNK, CHUNK), f32) * (0.5 * CHUNK ** -0.5)
    b_s = 1.0 + 0.02 * jax.random.normal(ks[7], (DEPTH, N_GMLP_HEADS, CHUNK), f32)
    w_out = jax.random.normal(ks[8], (DEPTH, D_MIX, D_MODEL), f32) * (0.5 * D_MIX ** -0.5)
    return {"x": x, "norm_g": norm_g, "w_in": w_in, "q_norm": q_norm, "k_norm": k_norm,
            "sinks": sinks, "w_s": w_s, "b_s": b_s, "w_out": w_out}


def reference(x, norm_g, w_in, q_norm, k_norm, sinks, w_s, b_s, w_out):
    B, S, _ = x.shape
    slopes = _alibi_slopes(N_Q_HEADS)
    sizes = [D_ATTN, D_KV, D_KV, D_ATTN, D_GMLP, D_GMLP, D_GMLP]
    cuts = [int(c) for c in np.cumsum(sizes)[:-1]]
    for l in range(DEPTH):
        h = _rmsnorm(x, norm_g[l])
        proj = jnp.einsum('bsd,de->bse', h, w_in[l])
        q, k, v, g_a, z_u, z_v, g_b = jnp.split(proj, cuts, axis=-1)
        q = _rmsnorm(q.reshape(B, S, N_Q_HEADS, HEAD_DIM), q_norm[l])
        k = _rmsnorm(k.reshape(B, S, N_KV_HEADS, HEAD_DIM), k_norm[l])
        v = v.reshape(B, S, N_KV_HEADS, HEAD_DIM)
        attn = _swa_gqa_sinks(q, k, v, sinks[l], slopes) * jax.nn.silu(g_a)
        z_u = jax.nn.gelu(z_u, approximate=False).reshape(B, S, N_GMLP_HEADS, HEAD_DIM)
        z_v = jax.nn.gelu(z_v, approximate=False).reshape(B, S, N_GMLP_HEADS, HEAD_DIM)
        sgu = _chunked_sgu(z_u, z_v, w_s[l], b_s[l]).reshape(B, S, D_GMLP) * jax.nn.silu(g_b)
        mix = jnp.concatenate([attn, sgu], axis=-1)
        x = x + jnp.einsum('bse,ed->bsd', mix, w_out[l])
    return x
```

```python
import functools

import jax
import jax.numpy as jnp
import numpy as np
from jax import lax
from jax.experimental import pallas as pl
from jax.experimental.pallas import tpu as pltpu

HEAD_DIM = 64
N_KV_HEADS = 4
WINDOW = 128
CHUNK = 128
RMS_EPS = 1e-6
LANES = 128
VMEM_LIMIT_BYTES = 56 * 1024 * 1024

F32 = jnp.float32
BF16 = jnp.bfloat16


def _compiler_params(n_grid_dims):
    return pltpu.CompilerParams(
        dimension_semantics=("arbitrary",) * n_grid_dims,
        vmem_limit_bytes=VMEM_LIMIT_BYTES,
    )


def _resident(block_shape, index_map):
    return pl.BlockSpec(block_shape, index_map, pipeline_mode=pl.Buffered(1))


def _head_rmsnorm(x, gain_tiled):
    rows, width = x.shape
    lane = lax.broadcasted_iota(jnp.int32, (rows, LANES), 1)
    low_half = lane < HEAD_DIM
    cols = []
    for c in range(width // LANES):
        blk = x[:, c * LANES:(c + 1) * LANES]
        sq = blk * blk
        ss_lo = jnp.sum(jnp.where(low_half, sq, 0.0), axis=-1, keepdims=True)
        ss_hi = jnp.sum(jnp.where(low_half, 0.0, sq), axis=-1, keepdims=True)
        r_lo = lax.rsqrt(ss_lo * (1.0 / HEAD_DIM) + RMS_EPS)
        r_hi = lax.rsqrt(ss_hi * (1.0 / HEAD_DIM) + RMS_EPS)
        cols.append(blk * jnp.where(low_half, r_lo, r_hi))
    return jnp.concatenate(cols, axis=-1) * gain_tiled


def _gelu_exact(x):
    return 0.5 * x * (1.0 + lax.erf(x * np.float32(1.0 / np.sqrt(2.0))))


def _silu(x):
    return x * jax.nn.sigmoid(x)


def _inproj_kernel(x_ref, g_ref, w_ref, qn_ref, kn_ref,
                   q_ref, k_ref, v_ref, ga_ref, u_ref, vs_ref, gb_ref,
                   *, d_attn, d_kv, d_gmlp):
    x = x_ref[...]
    ms = jnp.mean(x * x, axis=-1, keepdims=True)
    h = (x * lax.rsqrt(ms + RMS_EPS) * g_ref[...]).astype(BF16)

    def proj(lo, width):
        return jnp.dot(h, w_ref[:, lo:lo + width], preferred_element_type=F32)

    off = 0
    q = proj(off, d_attn); off += d_attn
    q_ref[...] = (_head_rmsnorm(q, qn_ref[...]) * (HEAD_DIM ** -0.5)).astype(BF16)
    k = proj(off, d_kv); off += d_kv
    k_ref[...] = _head_rmsnorm(k, kn_ref[...]).astype(BF16)
    v_ref[...] = proj(off, d_kv).astype(BF16); off += d_kv
    ga_ref[...] = _silu(proj(off, d_attn)).astype(BF16); off += d_attn
    u_ref[...] = _gelu_exact(proj(off, d_gmlp)).astype(BF16); off += d_gmlp
    vs_ref[...] = _gelu_exact(proj(off, d_gmlp)).astype(BF16); off += d_gmlp
    gb_ref[...] = _silu(proj(off, d_gmlp)).astype(BF16)


def _inproj(x2d, g, w_bf16, qn_tiled, kn_tiled, *, d_attn, d_kv, d_gmlp, tm):
    t, d_model = x2d.shape
    d_in = w_bf16.shape[1]
    row = lambda i: (i, 0)
    const = lambda i: (0, 0)
    widths = [d_attn, d_kv, d_kv, d_attn, d_gmlp, d_gmlp, d_gmlp]
    return pl.pallas_call(
        functools.partial(_inproj_kernel, d_attn=d_attn, d_kv=d_kv, d_gmlp=d_gmlp),
        grid=(t // tm,),
        in_specs=[
            pl.BlockSpec((tm, d_model), row),
            _resident((1, d_model), const),
            _resident((d_model, d_in), const),
            _resident((1, d_attn), const),
            _resident((1, d_kv), const),
        ],
        out_specs=[pl.BlockSpec((tm, w), row) for w in widths],
        out_shape=[jax.ShapeDtypeStruct((t, w), BF16) for w in widths],
        compiler_params=_compiler_params(1),
        name="inproj",
    )(x2d, g, w_bf16, qn_tiled, kn_tiled)


def _mixer_kernel(sinks_ref, q_ref, k_ref, kprev_ref, v_ref, vprev_ref,
                  ga_ref, u_ref, vs_ref, gb_ref, ws_ref, bfull_ref,
                  mix_ref,
                  kband, vband, bias, wtril,
                  *, n_q_heads, n_gmlp_heads, tb, d_attn):
    group = n_q_heads // N_KV_HEADS
    n_blocks = tb // WINDOW
    b_idx = pl.program_id(0)
    i_idx = pl.program_id(1)

    @pl.when((b_idx == 0) & (i_idx == 0))
    def _init_tables():
        qpos = lax.broadcasted_iota(jnp.int32, (WINDOW, 2 * WINDOW), 0) + WINDOW
        kpos = lax.broadcasted_iota(jnp.int32, (WINDOW, 2 * WINDOW), 1)
        dist = qpos - kpos
        in_window = (dist >= 0) & (dist < WINDOW)
        dist_f = dist.astype(F32)
        for j in range(n_q_heads):
            slope = np.float32(2.0 ** (-8.0 * (j + 1) / n_q_heads))
            alibi = -slope * dist_f
            bias[1, j] = jnp.where(in_window, alibi, -jnp.inf)
            bias[0, j] = jnp.where(in_window & (kpos >= WINDOW), alibi, -jnp.inf)
        row = lax.broadcasted_iota(jnp.int32, (CHUNK, CHUNK), 0)
        col = lax.broadcasted_iota(jnp.int32, (CHUNK, CHUNK), 1)
        for hh in range(n_gmlp_heads):
            wtril[hh] = jnp.where(row >= col, ws_ref[hh], 0.0).astype(BF16)

    kband[0:WINDOW, :] = kprev_ref[...]
    kband[WINDOW:, :] = k_ref[...]
    vband[0:WINDOW, :] = vprev_ref[...]
    vband[WINDOW:, :] = v_ref[...]

    lane = lax.broadcasted_iota(jnp.int32, (CHUNK, LANES), 1)
    low_half = lane < HEAD_DIM

    def block(jb, carry):
        r0 = pl.multiple_of(jb * WINDOW, WINDOW)
        has_prev = jnp.where((i_idx == 0) & (jb == 0), 0, 1)
        qb = q_ref[pl.ds(r0, WINDOW), :]
        kb = kband[pl.ds(r0, 2 * WINDOW), :]
        vb = vband[pl.ds(r0, 2 * WINDOW), :]
        outs = []
        for hk in range(N_KV_HEADS):
            kh = kb[:, hk * HEAD_DIM:(hk + 1) * HEAD_DIM]
            vh = vb[:, hk * HEAD_DIM:(hk + 1) * HEAD_DIM]
            for g in range(group):
                j = hk * group + g
                qg = qb[:, j * HEAD_DIM:(j + 1) * HEAD_DIM]
                s = lax.dot_general(qg, kh, (((1,), (1,)), ((), ())),
                                    preferred_element_type=F32)
                s = s + bias[has_prev, j]
                sink = sinks_ref[j]
                m = jnp.maximum(jnp.max(s, axis=-1, keepdims=True), sink)
                p = jnp.exp(s - m)
                den = jnp.sum(p, axis=-1, keepdims=True) + jnp.exp(sink - m)
                o = jnp.dot(p.astype(BF16), vh, preferred_element_type=F32)
                outs.append(o / den)
        attn = jnp.concatenate(outs, axis=-1)
        ga = ga_ref[pl.ds(r0, WINDOW), :].astype(F32)
        mix_ref[pl.ds(r0, WINDOW), 0:d_attn] = (attn * ga).astype(BF16)

        vs = vs_ref[pl.ds(r0, CHUNK), :]
        cols = []
        for jp in range(n_gmlp_heads // 2):
            vpair = vs[:, jp * LANES:(jp + 1) * LANES]
            o_lo = jnp.dot(wtril[2 * jp], vpair, preferred_element_type=F32)
            o_hi = jnp.dot(wtril[2 * jp + 1], vpair, preferred_element_type=F32)
            cols.append(jnp.where(low_half, o_lo, o_hi))
        mixed = jnp.concatenate(cols, axis=-1) + bfull_ref[...]
        u = u_ref[pl.ds(r0, CHUNK), :].astype(F32)
        gb = gb_ref[pl.ds(r0, CHUNK), :].astype(F32)
        mix_ref[pl.ds(r0, CHUNK), d_attn:] = (u * mixed * gb).astype(BF16)
        return carry

    lax.fori_loop(0, n_blocks, block, 0)


def _mixer(sinks, q, k, v, ga, u, vs, gb, w_s, b_full, *, batch, seq, tb):
    t, d_attn = q.shape
    d_kv = k.shape[1]
    d_gmlp = u.shape[1]
    n_q_heads = d_attn // HEAD_DIM
    n_gmlp_heads = d_gmlp // HEAD_DIM
    tiles = seq // tb
    blocks_per_tile = tb // WINDOW
    row = lambda b, i: (b * tiles + i, 0)
    prev = lambda b, i: (b * (seq // WINDOW) + jnp.maximum(i * blocks_per_tile - 1, 0), 0)
    return pl.pallas_call(
        functools.partial(_mixer_kernel, n_q_heads=n_q_heads, n_gmlp_heads=n_gmlp_heads,
                          tb=tb, d_attn=d_attn),
        grid=(batch, tiles),
        in_specs=[
            pl.BlockSpec(memory_space=pltpu.SMEM),
            pl.BlockSpec((tb, d_attn), row),
            pl.BlockSpec((tb, d_kv), row),
            pl.BlockSpec((WINDOW, d_kv), prev),
            pl.BlockSpec((tb, d_kv), row),
            pl.BlockSpec((WINDOW, d_kv), prev),
            pl.BlockSpec((tb, d_attn), row),
            pl.BlockSpec((tb, d_gmlp), row),
            pl.BlockSpec((tb, d_gmlp), row),
            pl.BlockSpec((tb, d_gmlp), row),
            _resident((n_gmlp_heads, CHUNK, CHUNK), lambda b, i: (0, 0, 0)),
            _resident((CHUNK, d_gmlp), lambda b, i: (0, 0)),
        ],
        out_specs=pl.BlockSpec((tb, d_attn + d_gmlp), row),
        out_shape=jax.ShapeDtypeStruct((t, d_attn + d_gmlp), BF16),
        scratch_shapes=[
            pltpu.VMEM((tb + WINDOW, d_kv), BF16),
            pltpu.VMEM((tb + WINDOW, d_kv), BF16),
            pltpu.VMEM((2, n_q_heads, WINDOW, 2 * WINDOW), F32),
            pltpu.VMEM((n_gmlp_heads, CHUNK, CHUNK), BF16),
        ],
        compiler_params=_compiler_params(2),
        name="mixer",
    )(sinks, q, k, k, v, v, ga, u, vs, gb, w_s, b_full)


def _outproj_kernel(x_ref, mix_ref, w_ref, o_ref):
    o_ref[...] = x_ref[...] + jnp.dot(mix_ref[...], w_ref[...], preferred_element_type=F32)


def _outproj(x2d, mix, w_bf16, *, tm):
    t, d_model = x2d.shape
    d_mix = mix.shape[1]
    row = lambda i: (i, 0)
    return pl.pallas_call(
        _outproj_kernel,
        grid=(t // tm,),
        in_specs=[
            pl.BlockSpec((tm, d_model), row),
            pl.BlockSpec((tm, d_mix), row),
            _resident((d_mix, d_model), lambda i: (0, 0)),
        ],
        out_specs=pl.BlockSpec((tm, d_model), row),
        out_shape=jax.ShapeDtypeStruct((t, d_model), F32),
        compiler_params=_compiler_params(1),
        name="outproj",
    )(x2d, mix, w_bf16)


def kernel(x, norm_g, w_in, q_norm, k_norm, sinks, w_s, b_s, w_out):
    batch, seq, d_model = x.shape
    depth = norm_g.shape[0]
    n_q_heads = sinks.shape[1]
    n_gmlp_heads = w_s.shape[1]
    d_attn = n_q_heads * HEAD_DIM
    d_gmlp = n_gmlp_heads * HEAD_DIM
    d_kv = N_KV_HEADS * HEAD_DIM
    assert w_in.shape[2] == 2 * d_attn + 2 * d_kv + 3 * d_gmlp
    assert seq % 512 == 0

    w_in_b = w_in.astype(BF16)
    w_out_b = w_out.astype(BF16)
    qn_tiled = jnp.tile(q_norm, (1, n_q_heads)).reshape(depth, 1, d_attn)
    kn_tiled = jnp.tile(k_norm, (1, N_KV_HEADS)).reshape(depth, 1, d_kv)
    b_full = jnp.repeat(jnp.swapaxes(b_s, 1, 2), HEAD_DIM, axis=2)

    x2d = x.reshape(batch * seq, d_model)
    for l in range(depth):
        q, k, v, ga, u, vs, gb = _inproj(
            x2d, norm_g[l].reshape(1, d_model), w_in_b[l], qn_tiled[l], kn_tiled[l],
            d_attn=d_attn, d_kv=d_kv, d_gmlp=d_gmlp, tm=512)
        mix = _mixer(sinks[l], q, k, v, ga, u, vs, gb, w_s[l], b_full[l],
                     batch=batch, seq=seq, tb=512)
        x2d = _outproj(x2d, mix, w_out_b[l], tm=512)
    return x2d.reshape(batch, seq, d_model)
```

```python
import functools

import jax
import jax.numpy as jnp
import numpy as np
from jax import lax
from jax.experimental import pallas as pl
from jax.experimental.pallas import tpu as pltpu

HEAD_DIM = 64
N_KV_HEADS = 4
WINDOW = 128
CHUNK = 128
RMS_EPS = 1e-6
LANES = 128
VMEM_LIMIT_BYTES = 56 * 1024 * 1024
TOKENS_PER_STEP = 512
LOG2_E = float(np.log2(np.e))

F32 = jnp.float32
BF16 = jnp.bfloat16
NT_DIMS = (((1,), (1,)), ((), ()))


def _compiler_params(n_grid_dims):
    return pltpu.CompilerParams(
        dimension_semantics=("arbitrary",) * n_grid_dims,
        vmem_limit_bytes=VMEM_LIMIT_BYTES,
    )


def _resident(block_shape, index_map):
    return pl.BlockSpec(block_shape, index_map, pipeline_mode=pl.Buffered(1))


def _low_half(rows):
    return lax.broadcasted_iota(jnp.int32, (rows, LANES), 1) < HEAD_DIM


def _head_rmsnorm(x, gain_tiled):
    rows, width = x.shape
    low_half = _low_half(rows)
    cols = []
    for c in range(width // LANES):
        blk = x[:, c * LANES:(c + 1) * LANES]
        sq = blk * blk
        ss_lo = jnp.sum(jnp.where(low_half, sq, 0.0), axis=-1, keepdims=True)
        ss_hi = jnp.sum(jnp.where(low_half, 0.0, sq), axis=-1, keepdims=True)
        r_lo = lax.rsqrt(ss_lo * (1.0 / HEAD_DIM) + RMS_EPS)
        r_hi = lax.rsqrt(ss_hi * (1.0 / HEAD_DIM) + RMS_EPS)
        cols.append(blk * jnp.where(low_half, r_lo, r_hi))
    return jnp.concatenate(cols, axis=-1) * gain_tiled


def _split_heads(x):
    rows, width = x.shape
    low_half = _low_half(rows)
    lo, hi = [], []
    for c in range(width // LANES):
        col = x[:, c * LANES:(c + 1) * LANES]
        swapped = pltpu.roll(col, HEAD_DIM, axis=1)
        lo += [jnp.where(low_half, col, 0.0), jnp.where(low_half, swapped, 0.0)]
        hi += [jnp.where(low_half, 0.0, swapped), jnp.where(low_half, 0.0, col)]
    return jnp.concatenate(lo, axis=-1), jnp.concatenate(hi, axis=-1)


def _gelu_exact(x):
    return 0.5 * x * (1.0 + lax.erf(x * np.float32(1.0 / np.sqrt(2.0))))


def _silu(x):
    return x * jax.nn.sigmoid(x)


def _inproj_kernel(x_ref, g_ref, w_ref, qn_ref, kn_ref,
                   q_ref, kx_ref, ky_ref, vx_ref, vy_ref, ga_ref, u_ref, vs_ref, gb_ref,
                   *, d_attn, d_kv, d_gmlp):
    x = x_ref[...]
    ms = jnp.mean(x * x, axis=-1, keepdims=True)
    h = (x * lax.rsqrt(ms + RMS_EPS) * g_ref[...]).astype(BF16)

    def proj(lo, width):
        return jnp.dot(h, w_ref[:, lo:lo + width], preferred_element_type=F32)

    off = 0
    q = proj(off, d_attn); off += d_attn
    q_ref[...] = (_head_rmsnorm(q, qn_ref[...]) * (HEAD_DIM ** -0.5 * LOG2_E)).astype(BF16)
    k = _head_rmsnorm(proj(off, d_kv), kn_ref[...]); off += d_kv
    kx, ky = _split_heads(k)
    kx_ref[...] = kx.astype(BF16)
    ky_ref[...] = ky.astype(BF16)
    vx, vy = _split_heads(proj(off, d_kv)); off += d_kv
    vx_ref[...] = vx.astype(BF16)
    vy_ref[...] = vy.astype(BF16)
    ga_ref[...] = _silu(proj(off, d_attn)).astype(BF16); off += d_attn
    u_ref[...] = _gelu_exact(proj(off, d_gmlp)).astype(BF16); off += d_gmlp
    vs_ref[...] = _gelu_exact(proj(off, d_gmlp)).astype(BF16); off += d_gmlp
    gb_ref[...] = _silu(proj(off, d_gmlp)).astype(BF16)


def _inproj(layer, x2d, norm_g, w_in_b, qn_tiled, kn_tiled, *, d_attn, d_kv, d_gmlp):
    t, d_model = x2d.shape
    d_in = w_in_b.shape[2]
    tm = TOKENS_PER_STEP
    row = lambda i: (i, 0)
    const = lambda i: (layer, 0, 0)
    widths = [d_attn, 2 * d_kv, 2 * d_kv, 2 * d_kv, 2 * d_kv, d_attn, d_gmlp, d_gmlp, d_gmlp]
    return pl.pallas_call(
        functools.partial(_inproj_kernel, d_attn=d_attn, d_kv=d_kv, d_gmlp=d_gmlp),
        grid=(t // tm,),
        in_specs=[
            pl.BlockSpec((tm, d_model), row),
            _resident((None, 1, d_model), const),
            _resident((None, d_model, d_in), const),
            _resident((None, 1, d_attn), const),
            _resident((None, 1, d_kv), const),
        ],
        out_specs=[pl.BlockSpec((tm, w), row) for w in widths],
        out_shape=[jax.ShapeDtypeStruct((t, w), BF16) for w in widths],
        compiler_params=_compiler_params(1),
        name="inproj",
    )(x2d, norm_g, w_in_b, qn_tiled, kn_tiled)


def _mixout_kernel(sinks_ref, x_ref, q_ref,
                   kx_ref, kxp_ref, ky_ref, kyp_ref, vx_ref, vxp_ref, vy_ref, vyp_ref,
                   ga_ref, u_ref, vs_ref, gb_ref, ws_ref, bfull_ref, wout_ref,
                   o_ref,
                   bias, wcat, ones, mix,
                   *, layer, n_q_heads, n_gmlp_heads, tb, d_attn):
    n_blocks = tb // WINDOW
    n_qcols = n_q_heads * HEAD_DIM // LANES
    qcols_per_kv = n_qcols // N_KV_HEADS
    first_tile = (pl.program_id(0) == 0) & (pl.program_id(1) == 0)

    @pl.when(first_tile)
    def _init_tables():
        qpos = lax.broadcasted_iota(jnp.int32, (WINDOW, 2 * WINDOW), 0) + WINDOW
        kpos = lax.broadcasted_iota(jnp.int32, (WINDOW, 2 * WINDOW), 1)
        dist = qpos - kpos
        in_window = (dist >= 0) & (dist < WINDOW)
        dist_f = dist.astype(F32)
        for j in range(n_q_heads):
            slope = np.float32(2.0 ** (-8.0 * (j + 1) / n_q_heads))
            alibi = (-slope * LOG2_E) * dist_f
            lanes = slice((j % 2) * 2 * WINDOW, (j % 2 + 1) * 2 * WINDOW)
            bias[1, j // 2, :, lanes] = jnp.where(in_window, alibi, -jnp.inf)
            bias[0, j // 2, :, lanes] = jnp.where(in_window & (kpos >= WINDOW), alibi, -jnp.inf)
        row = lax.broadcasted_iota(jnp.int32, (CHUNK, CHUNK), 0)
        col = lax.broadcasted_iota(jnp.int32, (CHUNK, CHUNK), 1)
        for hh in range(n_gmlp_heads):
            wcat[hh // 2, :, (hh % 2) * CHUNK:(hh % 2 + 1) * CHUNK] = (
                jnp.where(row >= col, ws_ref[hh], 0.0).astype(BF16))
        r = lax.broadcasted_iota(jnp.int32, (4 * WINDOW, LANES), 0)
        c = lax.broadcasted_iota(jnp.int32, (4 * WINDOW, LANES), 1)
        ones[...] = jnp.where((r < 2 * WINDOW) == (c < HEAD_DIM), 1.0, 0.0).astype(BF16)

    low_half = _low_half(WINDOW)
    has_prev0 = jnp.where(pl.program_id(1) == 0, 0, 1)

    def band(own_ref, prev_ref, jb, lanes):
        prev = prev_ref[:, lanes] if jb == 0 else own_ref[(jb - 1) * WINDOW:jb * WINDOW, lanes]
        return [prev, own_ref[jb * WINDOW:(jb + 1) * WINDOW, lanes]]

    def mix_block(jb):
        rows = slice(jb * WINDOW, (jb + 1) * WINDOW)
        for hk in range(N_KV_HEADS):
            lanes = slice(hk * LANES, (hk + 1) * LANES)
            krhs = jnp.concatenate(band(kx_ref, kxp_ref, jb, lanes) + band(ky_ref, kyp_ref, jb, lanes),
                                   axis=0)
            vstack = jnp.concatenate(band(vx_ref, vxp_ref, jb, lanes) + band(vy_ref, vyp_ref, jb, lanes),
                                     axis=0)
            vrhs = jnp.concatenate([vstack, ones[...]], axis=1)
            for pr in range(qcols_per_kv):
                c = hk * qcols_per_kv + pr
                cl = slice(c * LANES, (c + 1) * LANES)
                sink_a = sinks_ref[layer, 2 * c] * LOG2_E
                sink_b = sinks_ref[layer, 2 * c + 1] * LOG2_E
                s = lax.dot_general(q_ref[rows, cl], krhs, NT_DIMS, preferred_element_type=F32)
                s = s + (bias[has_prev0, c] if jb == 0 else bias[1, c])
                s_a = s[:, :2 * WINDOW]
                s_b = s[:, 2 * WINDOW:]
                m_a = jnp.maximum(jnp.max(s_a, axis=-1, keepdims=True), sink_a)
                m_b = jnp.maximum(jnp.max(s_b, axis=-1, keepdims=True), sink_b)
                p = jnp.concatenate([jnp.exp2(s_a - m_a), jnp.exp2(s_b - m_b)], axis=1)
                o = jnp.dot(p.astype(BF16), vrhs, preferred_element_type=F32)
                sink_term = jnp.where(low_half, jnp.exp2(sink_a - m_a), jnp.exp2(sink_b - m_b))
                attn = o[:, :LANES] / (o[:, LANES:] + sink_term)
                mix[rows, cl] = (attn * ga_ref[rows, cl].astype(F32)).astype(BF16)

        for jp in range(n_gmlp_heads // 2):
            cl = slice(jp * LANES, (jp + 1) * LANES)
            vpair = vs_ref[rows, cl]
            zero = jnp.zeros_like(vpair)
            rhs = jnp.concatenate([jnp.where(low_half, vpair, zero),
                                   jnp.where(low_half, zero, vpair)], axis=0)
            mixed = jnp.dot(wcat[jp], rhs, preferred_element_type=F32) + bfull_ref[:, cl]
            sgu = u_ref[rows, cl].astype(F32) * mixed * gb_ref[rows, cl].astype(F32)
            mix[rows, d_attn + jp * LANES:d_attn + (jp + 1) * LANES] = sgu.astype(BF16)

    def out_block(jb):
        rows = slice(jb * WINDOW, (jb + 1) * WINDOW)
        o_ref[rows, :] = x_ref[rows, :] + jnp.dot(mix[rows, :], wout_ref[...],
                                                  preferred_element_type=F32)

    mix_block(0)
    for jb in range(1, n_blocks):
        mix_block(jb)
        out_block(jb - 1)
    out_block(n_blocks - 1)


def _mixout(layer, sinks, x2d, q, kx, ky, vx, vy, ga, u, vs, gb, w_s, b_full, w_out_b,
            *, batch, seq):
    t, d_model = x2d.shape
    d_attn = q.shape[1]
    d_kvx = kx.shape[1]
    d_gmlp = u.shape[1]
    n_q_heads = d_attn // HEAD_DIM
    n_gmlp_heads = d_gmlp // HEAD_DIM
    tb = TOKENS_PER_STEP
    tiles = seq // tb
    blocks_per_tile = tb // WINDOW
    row = lambda b, i: (b * tiles + i, 0)
    prev = lambda b, i: (b * (seq // WINDOW) + jnp.maximum(i * blocks_per_tile - 1, 0), 0)
    own_kv = pl.BlockSpec((tb, d_kvx), row)
    prev_kv = pl.BlockSpec((WINDOW, d_kvx), prev)
    return pl.pallas_call(
        functools.partial(_mixout_kernel, layer=layer, n_q_heads=n_q_heads,
                          n_gmlp_heads=n_gmlp_heads, tb=tb, d_attn=d_attn),
        grid=(batch, tiles),
        in_specs=[
            pl.BlockSpec(memory_space=pltpu.SMEM),
            pl.BlockSpec((tb, d_model), row),
            pl.BlockSpec((tb, d_attn), row),
            own_kv, prev_kv, own_kv, prev_kv, own_kv, prev_kv, own_kv, prev_kv,
            pl.BlockSpec((tb, d_attn), row),
            pl.BlockSpec((tb, d_gmlp), row),
            pl.BlockSpec((tb, d_gmlp), row),
            pl.BlockSpec((tb, d_gmlp), row),
            _resident((None, n_gmlp_heads, CHUNK, CHUNK), lambda b, i: (layer, 0, 0, 0)),
            _resident((None, CHUNK, d_gmlp), lambda b, i: (layer, 0, 0)),
            _resident((None, d_attn + d_gmlp, d_model), lambda b, i: (layer, 0, 0)),
        ],
        out_specs=pl.BlockSpec((tb, d_model), row),
        out_shape=jax.ShapeDtypeStruct((t, d_model), F32),
        scratch_shapes=[
            pltpu.VMEM((2, n_q_heads // 2, WINDOW, 4 * WINDOW), F32),
            pltpu.VMEM((n_gmlp_heads // 2, CHUNK, 2 * CHUNK), BF16),
            pltpu.VMEM((4 * WINDOW, LANES), BF16),
            pltpu.VMEM((tb, d_attn + d_gmlp), BF16),
        ],
        compiler_params=_compiler_params(2),
        name="mixout",
    )(sinks, x2d, q, kx, kx, ky, ky, vx, vx, vy, vy, ga, u, vs, gb, w_s, b_full, w_out_b)


def kernel(x, norm_g, w_in, q_norm, k_norm, sinks, w_s, b_s, w_out):
    batch, seq, d_model = x.shape
    depth = norm_g.shape[0]
    n_q_heads = sinks.shape[1]
    n_gmlp_heads = w_s.shape[1]
    d_attn = n_q_heads * HEAD_DIM
    d_gmlp = n_gmlp_heads * HEAD_DIM
    d_kv = N_KV_HEADS * HEAD_DIM
    assert w_in.shape[2] == 2 * d_attn + 2 * d_kv + 3 * d_gmlp
    assert seq % TOKENS_PER_STEP == 0 and q_norm.shape[1] == HEAD_DIM

    w_in_b = w_in.astype(BF16)
    w_out_b = w_out.astype(BF16)
    norm_g3 = norm_g.reshape(depth, 1, d_model)
    qn_tiled = jnp.tile(q_norm, (1, n_q_heads)).reshape(depth, 1, d_attn)
    kn_tiled = jnp.tile(k_norm, (1, N_KV_HEADS)).reshape(depth, 1, d_kv)
    b_full = jnp.repeat(jnp.swapaxes(b_s, 1, 2), HEAD_DIM, axis=2)

    x2d = x.reshape(batch * seq, d_model)
    for l in range(depth):
        q, kx, ky, vx, vy, ga, u, vs, gb = _inproj(
            l, x2d, norm_g3, w_in_b, qn_tiled, kn_tiled, d_attn=d_attn, d_kv=d_kv, d_gmlp=d_gmlp)
        x2d = _mixout(l, sinks, x2d, q, kx, ky, vx, vy, ga, u, vs, gb, w_s, b_full, w_out_b,
                      batch=batch, seq=seq)
    return x2d.reshape(batch, seq, d_model)
```

```python
import functools

import jax
import jax.numpy as jnp
import numpy as np
from jax import lax
from jax.experimental import pallas as pl
from jax.experimental.pallas import tpu as pltpu

HEAD_DIM = 64
N_KV_HEADS = 4
WINDOW = 128
CHUNK = 128
RMS_EPS = 1e-6
LANES = 128
VMEM_LIMIT_BYTES = 56 * 1024 * 1024
TOKENS_PER_STEP = 512
OUT_ROWS = 256
LOG2_E = float(np.log2(np.e))

F32 = jnp.float32
BF16 = jnp.bfloat16


def _compiler_params(n_grid_dims):
    return pltpu.CompilerParams(
        dimension_semantics=("arbitrary",) * n_grid_dims,
        vmem_limit_bytes=VMEM_LIMIT_BYTES,
    )


def _resident(block_shape, index_map):
    return pl.BlockSpec(block_shape, index_map, pipeline_mode=pl.Buffered(1))


def _low_half(rows):
    return lax.broadcasted_iota(jnp.int32, (rows, LANES), 1) < HEAD_DIM


def _head_rmsnorm(x, gain_tiled):
    rows, width = x.shape
    low_half = _low_half(rows)
    cols = []
    for c in range(width // LANES):
        blk = x[:, c * LANES:(c + 1) * LANES]
        sq = blk * blk
        ss_lo = jnp.sum(jnp.where(low_half, sq, 0.0), axis=-1, keepdims=True)
        ss_hi = jnp.sum(jnp.where(low_half, 0.0, sq), axis=-1, keepdims=True)
        r_lo = lax.rsqrt(ss_lo * (1.0 / HEAD_DIM) + RMS_EPS)
        r_hi = lax.rsqrt(ss_hi * (1.0 / HEAD_DIM) + RMS_EPS)
        cols.append(blk * jnp.where(low_half, r_lo, r_hi))
    return jnp.concatenate(cols, axis=-1) * gain_tiled


def _split_heads(x):
    rows, width = x.shape
    low_half = _low_half(rows)
    lo, hi = [], []
    for c in range(width // LANES):
        col = x[:, c * LANES:(c + 1) * LANES]
        swapped = pltpu.roll(col, HEAD_DIM, axis=1)
        lo += [jnp.where(low_half, col, 0.0), jnp.where(low_half, swapped, 0.0)]
        hi += [jnp.where(low_half, 0.0, swapped), jnp.where(low_half, 0.0, col)]
    return jnp.concatenate(lo, axis=-1), jnp.concatenate(hi, axis=-1)


def _gelu_exact(x):
    return 0.5 * x * (1.0 + lax.erf(x * np.float32(1.0 / np.sqrt(2.0))))


def _silu(x):
    half = 0.5 * x
    return half + half * jnp.tanh(half)


def _inproj_kernel(x_ref, g_ref, w_ref, qn_ref, kn_ref,
                   q_ref, kt_ref, vx_ref, vy_ref, ga_ref, u_ref, vs_ref, gb_ref,
                   *, d_attn, d_kv, d_gmlp):
    x = x_ref[...]
    xg = (x * g_ref[...]).astype(BF16)
    r = lax.rsqrt(jnp.mean(x * x, axis=-1, keepdims=True) + RMS_EPS)

    def proj(lo, width):
        return r * jnp.dot(xg, w_ref[:, lo:lo + width], preferred_element_type=F32)

    q_off, k_off, v_off = 0, d_attn, d_attn + d_kv
    ga_off = v_off + d_kv
    u_off, vs_off, gb_off = ga_off + d_attn, ga_off + d_attn + d_gmlp, ga_off + d_attn + 2 * d_gmlp

    q = _head_rmsnorm(proj(q_off, d_attn), qn_ref[...])
    q_ref[...] = (q * (HEAD_DIM ** -0.5 * LOG2_E)).astype(BF16)
    k = _head_rmsnorm(proj(k_off, d_kv), kn_ref[...])
    for jb in range(k.shape[0] // WINDOW):
        kt_ref[jb * d_kv:(jb + 1) * d_kv, :] = k[jb * WINDOW:(jb + 1) * WINDOW, :].T.astype(BF16)
    vx, vy = _split_heads(proj(v_off, d_kv))
    vx_ref[...] = vx.astype(BF16)
    vy_ref[...] = vy.astype(BF16)
    u_ref[...] = _gelu_exact(proj(u_off, d_gmlp)).astype(BF16)
    vs_ref[...] = _gelu_exact(proj(vs_off, d_gmlp)).astype(BF16)
    ga_ref[...] = _silu(proj(ga_off, d_attn)).astype(BF16)
    gb_ref[...] = _silu(proj(gb_off, d_gmlp)).astype(BF16)


def _inproj(layer, x2d, norm_g, w_in_b, qn_tiled, kn_tiled, *, d_attn, d_kv, d_gmlp):
    t, d_model = x2d.shape
    d_in = w_in_b.shape[2]
    tm = TOKENS_PER_STEP
    row = lambda i: (i, 0)
    const = lambda i: (layer, 0, 0)
    widths = [d_attn, 2 * d_kv, 2 * d_kv, d_attn, d_gmlp, d_gmlp, d_gmlp]
    bf16_rows = lambda w: (pl.BlockSpec((tm, w), row), jax.ShapeDtypeStruct((t, w), BF16))
    outs = [bf16_rows(d_attn),
            (pl.BlockSpec((tm // WINDOW * d_kv, WINDOW), row),
             jax.ShapeDtypeStruct((t // WINDOW * d_kv, WINDOW), BF16))]
    outs += [bf16_rows(w) for w in widths[1:]]
    return pl.pallas_call(
        functools.partial(_inproj_kernel, d_attn=d_attn, d_kv=d_kv, d_gmlp=d_gmlp),
        grid=(t // tm,),
        in_specs=[
            pl.BlockSpec((tm, d_model), row),
            _resident((None, 1, d_model), const),
            _resident((None, d_model, d_in), const),
            _resident((None, 1, d_attn), const),
            _resident((None, 1, d_kv), const),
        ],
        out_specs=[o[0] for o in outs],
        out_shape=[o[1] for o in outs],
        compiler_params=_compiler_params(1),
        name="inproj",
    )(x2d, norm_g, w_in_b, qn_tiled, kn_tiled)


def _mixout_kernel(sinks_ref, x_ref, q_ref, kt_ref, ktp_ref, vx_ref, vxp_ref, vy_ref, vyp_ref,
                   ga_ref, u_ref, vs_ref, gb_ref, ws_ref, bfull_ref, wout_ref,
                   o_ref,
                   bias, wcat, ones, mix,
                   *, layer, n_q_heads, n_gmlp_heads, tb, d_attn):
    n_blocks = tb // WINDOW
    d_kv = N_KV_HEADS * HEAD_DIM
    first_tile = (pl.program_id(0) == 0) & (pl.program_id(1) == 0)

    @pl.when(first_tile)
    def _init_tables():
        qpos = lax.broadcasted_iota(jnp.int32, (WINDOW, 2 * WINDOW), 0) + WINDOW
        kpos = lax.broadcasted_iota(jnp.int32, (WINDOW, 2 * WINDOW), 1)
        dist = qpos - kpos
        in_window = (dist >= 0) & (dist < WINDOW)
        dist_f = dist.astype(F32)
        for j in range(n_q_heads):
            slope = np.float32(2.0 ** (-8.0 * (j + 1) / n_q_heads))
            alibi = (-slope * LOG2_E) * dist_f
            rows = slice((j // 2 % 2) * WINDOW, (j // 2 % 2 + 1) * WINDOW)
            lanes = slice((j % 2) * 2 * WINDOW, (j % 2 + 1) * 2 * WINDOW)
            bias[1, j // 4, rows, lanes] = jnp.where(in_window, alibi, -jnp.inf)
            bias[0, j // 4, rows, lanes] = jnp.where(in_window & (kpos >= WINDOW), alibi, -jnp.inf)
        row = lax.broadcasted_iota(jnp.int32, (CHUNK, CHUNK), 0)
        col = lax.broadcasted_iota(jnp.int32, (CHUNK, CHUNK), 1)
        for hh in range(n_gmlp_heads):
            wcat[hh // 2, :, (hh % 2) * CHUNK:(hh % 2 + 1) * CHUNK] = (
                jnp.where(row >= col, ws_ref[hh], 0.0).astype(BF16))
        r = lax.broadcasted_iota(jnp.int32, (4 * WINDOW, LANES), 0)
        c = lax.broadcasted_iota(jnp.int32, (4 * WINDOW, LANES), 1)
        ones[...] = jnp.where((r < 2 * WINDOW) == (c < HEAD_DIM), 1.0, 0.0).astype(BF16)

    low_half = _low_half(WINDOW)
    has_prev0 = jnp.where(pl.program_id(1) == 0, 0, 1)
    zeros_kt = jnp.zeros((HEAD_DIM, 2 * WINDOW), BF16)

    def band(own_ref, prev_ref, jb, lanes):
        prev = prev_ref[:, lanes] if jb == 0 else own_ref[(jb - 1) * WINDOW:jb * WINDOW, lanes]
        return [prev, own_ref[jb * WINDOW:(jb + 1) * WINDOW, lanes]]

    def mix_block(jb):
        rows = slice(jb * WINDOW, (jb + 1) * WINDOW)
        for hk in range(N_KV_HEADS):
            feat = lambda blk: slice(blk * d_kv + hk * HEAD_DIM, blk * d_kv + (hk + 1) * HEAD_DIM)
            own_kt = kt_ref[feat(jb), :]
            prev_kt = ktp_ref[feat(0), :] if jb == 0 else kt_ref[feat(jb - 1), :]
            kband = jnp.concatenate([prev_kt, own_kt], axis=1)
            krhs = jnp.concatenate([jnp.concatenate([kband, zeros_kt], axis=1),
                                    jnp.concatenate([zeros_kt, kband], axis=1)], axis=0)
            lanes = slice(hk * LANES, (hk + 1) * LANES)
            vstack = jnp.concatenate(band(vx_ref, vxp_ref, jb, lanes) + band(vy_ref, vyp_ref, jb, lanes),
                                     axis=0)
            vrhs = jnp.concatenate([vstack, ones[...]], axis=1)
            cols = [slice((2 * hk + pr) * LANES, (2 * hk + pr + 1) * LANES) for pr in range(2)]
            q2 = jnp.concatenate([q_ref[rows, cl] for cl in cols], axis=0)
            s = jnp.dot(q2, krhs, preferred_element_type=F32)
            s = s + (bias[has_prev0, hk] if jb == 0 else bias[1, hk])
            probs, sink_terms = [], []
            for pr in range(2):
                sink_a = sinks_ref[layer, 4 * hk + 2 * pr] * LOG2_E
                sink_b = sinks_ref[layer, 4 * hk + 2 * pr + 1] * LOG2_E
                s_a = s[pr * WINDOW:(pr + 1) * WINDOW, :2 * WINDOW]
                s_b = s[pr * WINDOW:(pr + 1) * WINDOW, 2 * WINDOW:]
                m_a = jnp.maximum(jnp.max(s_a, axis=-1, keepdims=True), sink_a)
                m_b = jnp.maximum(jnp.max(s_b, axis=-1, keepdims=True), sink_b)
                probs.append(jnp.concatenate([jnp.exp2(s_a - m_a), jnp.exp2(s_b - m_b)],
                                             axis=1).astype(BF16))
                sink_terms.append(jnp.where(low_half, jnp.exp2(sink_a - m_a),
                                            jnp.exp2(sink_b - m_b)))
            o = jnp.dot(jnp.concatenate(probs, axis=0), vrhs,
                        preferred_element_type=F32)
            for pr in range(2):
                o_pr = o[pr * WINDOW:(pr + 1) * WINDOW, :]
                attn = o_pr[:, :LANES] / (o_pr[:, LANES:] + sink_terms[pr])
                mix[rows, cols[pr]] = (attn * ga_ref[rows, cols[pr]].astype(F32)).astype(BF16)

        for jp in range(n_gmlp_heads // 2):
            cl = slice(jp * LANES, (jp + 1) * LANES)
            vpair = vs_ref[rows, cl]
            zero = jnp.zeros_like(vpair)
            rhs = jnp.concatenate([jnp.where(low_half, vpair, zero),
                                   jnp.where(low_half, zero, vpair)], axis=0)
            mixed = jnp.dot(wcat[jp], rhs, preferred_element_type=F32) + bfull_ref[:, cl]
            sgu = u_ref[rows, cl].astype(F32) * mixed * gb_ref[rows, cl].astype(F32)
            mix[rows, d_attn + jp * LANES:d_attn + (jp + 1) * LANES] = sgu.astype(BF16)

    def out_rows(jo):
        rows = slice(jo * OUT_ROWS, (jo + 1) * OUT_ROWS)
        o_ref[rows, :] = x_ref[rows, :] + jnp.dot(mix[rows, :], wout_ref[...],
                                                  preferred_element_type=F32)

    blocks_per_out = OUT_ROWS // WINDOW
    for jb in range(n_blocks):
        mix_block(jb)
        if (jb + 1) % blocks_per_out == 0:
            out_rows(jb // blocks_per_out)


def _mixout(layer, sinks, x2d, q, kt, vx, vy, ga, u, vs, gb, w_s, b_full, w_out_b,
            *, batch, seq):
    t, d_model = x2d.shape
    d_attn = q.shape[1]
    d_kvx = vx.shape[1]
    d_kv = N_KV_HEADS * HEAD_DIM
    d_gmlp = u.shape[1]
    n_q_heads = d_attn // HEAD_DIM
    n_gmlp_heads = d_gmlp // HEAD_DIM
    tb = TOKENS_PER_STEP
    tiles = seq // tb
    blocks_per_tile = tb // WINDOW
    row = lambda b, i: (b * tiles + i, 0)
    prev = lambda b, i: (b * (seq // WINDOW) + jnp.maximum(i * blocks_per_tile - 1, 0), 0)
    own_kv = pl.BlockSpec((tb, d_kvx), row)
    prev_kv = pl.BlockSpec((WINDOW, d_kvx), prev)
    return pl.pallas_call(
        functools.partial(_mixout_kernel, layer=layer, n_q_heads=n_q_heads,
                          n_gmlp_heads=n_gmlp_heads, tb=tb, d_attn=d_attn),
        grid=(batch, tiles),
        in_specs=[
            pl.BlockSpec(memory_space=pltpu.SMEM),
            pl.BlockSpec((tb, d_model), row),
            pl.BlockSpec((tb, d_attn), row),
            pl.BlockSpec((blocks_per_tile * d_kv, WINDOW), row),
            pl.BlockSpec((d_kv, WINDOW), prev),
            own_kv, prev_kv, own_kv, prev_kv,
            pl.BlockSpec((tb, d_attn), row),
            pl.BlockSpec((tb, d_gmlp), row),
            pl.BlockSpec((tb, d_gmlp), row),
            pl.BlockSpec((tb, d_gmlp), row),
            _resident((None, n_gmlp_heads, CHUNK, CHUNK), lambda b, i: (layer, 0, 0, 0)),
            _resident((None, CHUNK, d_gmlp), lambda b, i: (layer, 0, 0)),
            _resident((None, d_attn + d_gmlp, d_model), lambda b, i: (layer, 0, 0)),
        ],
        out_specs=pl.BlockSpec((tb, d_model), row),
        out_shape=jax.ShapeDtypeStruct((t, d_model), F32),
        scratch_shapes=[
            pltpu.VMEM((2, N_KV_HEADS, 2 * WINDOW, 4 * WINDOW), F32),
            pltpu.VMEM((n_gmlp_heads // 2, CHUNK, 2 * CHUNK), BF16),
            pltpu.VMEM((4 * WINDOW, LANES), BF16),
            pltpu.VMEM((tb, d_attn + d_gmlp), BF16),
        ],
        compiler_params=_compiler_params(2),
        name="mixout",
    )(sinks, x2d, q, kt, kt, vx, vx, vy, vy, ga, u, vs, gb, w_s, b_full, w_out_b)


def kernel(x, norm_g, w_in, q_norm, k_norm, sinks, w_s, b_s, w_out):
    batch, seq, d_model = x.shape
    depth = norm_g.shape[0]
    n_q_heads = sinks.shape[1]
    n_gmlp_heads = w_s.shape[1]
    d_attn = n_q_heads * HEAD_DIM
    d_gmlp = n_gmlp_heads * HEAD_DIM
    d_kv = N_KV_HEADS * HEAD_DIM
    assert w_in.shape[2] == 2 * d_attn + 2 * d_kv + 3 * d_gmlp
    assert seq % TOKENS_PER_STEP == 0 and q_norm.shape[1] == HEAD_DIM
    assert n_q_heads == 4 * N_KV_HEADS and TOKENS_PER_STEP % OUT_ROWS == 0

    w_in_b = w_in.astype(BF16)
    w_out_b = w_out.astype(BF16)
    norm_g3 = norm_g.reshape(depth, 1, d_model)
    qn_tiled = jnp.tile(q_norm, (1, n_q_heads)).reshape(depth, 1, d_attn)
    kn_tiled = jnp.tile(k_norm, (1, N_KV_HEADS)).reshape(depth, 1, d_kv)
    b_full = jnp.repeat(jnp.swapaxes(b_s, 1, 2), HEAD_DIM, axis=2)

    x2d = x.reshape(batch * seq, d_model)
    for l in range(depth):
        q, kt, vx, vy, ga, u, vs, gb = _inproj(
            l, x2d, norm_g3, w_in_b, qn_tiled, kn_tiled, d_attn=d_attn, d_kv=d_kv, d_gmlp=d_gmlp)
        x2d = _mixout(l, sinks, x2d, q, kt, vx, vy, ga, u, vs, gb, w_s, b_full, w_out_b,
                      batch=batch, seq=seq)
    return x2d.reshape(batch, seq, d_model)
```

```python
import collections
import functools

import jax
import jax.numpy as jnp
import numpy as np
from jax import lax
from jax.experimental import pallas as pl
from jax.experimental.pallas import tpu as pltpu

HEAD_DIM = 64
N_KV_HEADS = 4
WINDOW = 128
CHUNK = 128
RMS_EPS = 1e-6
LANES = 128
VMEM_LIMIT_BYTES = 56 * 1024 * 1024
TOKENS_PER_STEP = 512
OUT_ROWS = 256
LOG2_E = float(np.log2(np.e))

F32 = jnp.float32
BF16 = jnp.bfloat16

ActLayout = collections.namedtuple("ActLayout", "kt vx vy q ga u vs gb width band_width")


def _act_layout(d_attn, d_kv, d_gmlp):
    kt, vx = 0, d_kv
    vy = vx + 2 * d_kv
    q = vy + 2 * d_kv
    ga = q + d_attn
    u = ga + d_attn
    vs = u + d_gmlp
    gb = vs + d_gmlp
    return ActLayout(kt, vx, vy, q, ga, u, vs, gb, width=gb + d_gmlp, band_width=q)


def _compiler_params(n_grid_dims):
    return pltpu.CompilerParams(
        dimension_semantics=("arbitrary",) * n_grid_dims,
        vmem_limit_bytes=VMEM_LIMIT_BYTES,
    )


def _resident(block_shape, index_map):
    return pl.BlockSpec(block_shape, index_map, pipeline_mode=pl.Buffered(1))


def _low_half(rows):
    return lax.broadcasted_iota(jnp.int32, (rows, LANES), 1) < HEAD_DIM


def _head_rmsnorm(x, gain_tiled):
    rows, width = x.shape
    low_half = _low_half(rows)
    cols = []
    for c in range(width // LANES):
        blk = x[:, c * LANES:(c + 1) * LANES]
        sq = blk * blk
        ss_lo = jnp.sum(jnp.where(low_half, sq, 0.0), axis=-1, keepdims=True)
        ss_hi = jnp.sum(jnp.where(low_half, 0.0, sq), axis=-1, keepdims=True)
        r_lo = lax.rsqrt(ss_lo * (1.0 / HEAD_DIM) + RMS_EPS)
        r_hi = lax.rsqrt(ss_hi * (1.0 / HEAD_DIM) + RMS_EPS)
        cols.append(blk * jnp.where(low_half, r_lo, r_hi))
    return jnp.concatenate(cols, axis=-1) * gain_tiled


def _split_heads(x):
    rows, width = x.shape
    low_half = _low_half(rows)
    lo, hi = [], []
    for c in range(width // LANES):
        col = x[:, c * LANES:(c + 1) * LANES]
        swapped = pltpu.roll(col, HEAD_DIM, axis=1)
        lo += [jnp.where(low_half, col, 0.0), jnp.where(low_half, swapped, 0.0)]
        hi += [jnp.where(low_half, 0.0, swapped), jnp.where(low_half, 0.0, col)]
    return jnp.concatenate(lo, axis=-1), jnp.concatenate(hi, axis=-1)


def _gelu_exact(x):
    return 0.5 * x * (1.0 + lax.erf(x * np.float32(1.0 / np.sqrt(2.0))))


def _silu(x):
    half = 0.5 * x
    return half + half * jnp.tanh(half)


def _inproj_kernel(x_ref, g_ref, w_ref, qn_ref, kn_ref, act_ref, *, d_attn, d_kv, d_gmlp):
    lay = _act_layout(d_attn, d_kv, d_gmlp)
    x = x_ref[...]
    xg = (x * g_ref[...]).astype(BF16)
    r = lax.rsqrt(jnp.mean(x * x, axis=-1, keepdims=True) + RMS_EPS)

    def proj(lo, width):
        return r * jnp.dot(xg, w_ref[:, lo:lo + width], preferred_element_type=F32)

    def put(off, value):
        act_ref[:, off:off + value.shape[1]] = value.astype(BF16)

    q_off, k_off, v_off = 0, d_attn, d_attn + d_kv
    ga_off = v_off + d_kv
    u_off, vs_off, gb_off = ga_off + d_attn, ga_off + d_attn + d_gmlp, ga_off + d_attn + 2 * d_gmlp

    q = _head_rmsnorm(proj(q_off, d_attn), qn_ref[...])
    put(lay.q, q * (HEAD_DIM ** -0.5 * LOG2_E))
    k = _head_rmsnorm(proj(k_off, d_kv), kn_ref[...])
    for jb in range(k.shape[0] // WINDOW):
        for c in range(d_kv // LANES):
            tile = k[jb * WINDOW:(jb + 1) * WINDOW, c * LANES:(c + 1) * LANES]
            act_ref[jb * WINDOW:(jb + 1) * WINDOW,
                    lay.kt + c * LANES:lay.kt + (c + 1) * LANES] = tile.T.astype(BF16)
    vx, vy = _split_heads(proj(v_off, d_kv))
    put(lay.vx, vx)
    put(lay.vy, vy)
    put(lay.u, _gelu_exact(proj(u_off, d_gmlp)))
    put(lay.vs, _gelu_exact(proj(vs_off, d_gmlp)))
    put(lay.ga, _silu(proj(ga_off, d_attn)))
    put(lay.gb, _silu(proj(gb_off, d_gmlp)))


def _inproj(layer, x2d, norm_g, w_in_b, qn_tiled, kn_tiled, *, d_attn, d_kv, d_gmlp):
    t, d_model = x2d.shape
    d_in = w_in_b.shape[2]
    tm = TOKENS_PER_STEP
    width = _act_layout(d_attn, d_kv, d_gmlp).width
    row = lambda i: (i, 0)
    const = lambda i: (layer, 0, 0)
    return pl.pallas_call(
        functools.partial(_inproj_kernel, d_attn=d_attn, d_kv=d_kv, d_gmlp=d_gmlp),
        grid=(t // tm,),
        in_specs=[
            pl.BlockSpec((tm, d_model), row),
            _resident((None, 1, d_model), const),
            _resident((None, d_model, d_in), const),
            _resident((None, 1, d_attn), const),
            _resident((None, 1, d_kv), const),
        ],
        out_specs=pl.BlockSpec((tm, width), row),
        out_shape=jax.ShapeDtypeStruct((t, width), BF16),
        compiler_params=_compiler_params(1),
        name="inproj",
    )(x2d, norm_g, w_in_b, qn_tiled, kn_tiled)


def _mixout_kernel(sinks_ref, x_ref, act_ref, prev_ref, ws_ref, bfull_ref, wout_ref,
                   o_ref,
                   bias, wcat, ones, mix,
                   *, layer, n_q_heads, n_gmlp_heads, tb, d_attn):
    n_blocks = tb // WINDOW
    d_kv = N_KV_HEADS * HEAD_DIM
    d_gmlp = n_gmlp_heads * HEAD_DIM
    lay = _act_layout(d_attn, d_kv, d_gmlp)
    first_tile = (pl.program_id(0) == 0) & (pl.program_id(1) == 0)

    @pl.when(first_tile)
    def _init_tables():
        qpos = lax.broadcasted_iota(jnp.int32, (WINDOW, 2 * WINDOW), 0) + WINDOW
        kpos = lax.broadcasted_iota(jnp.int32, (WINDOW, 2 * WINDOW), 1)
        dist = qpos - kpos
        in_window = (dist >= 0) & (dist < WINDOW)
        dist_f = dist.astype(F32)
        for j in range(n_q_heads):
            slope = np.float32(2.0 ** (-8.0 * (j + 1) / n_q_heads))
            alibi = (-slope * LOG2_E) * dist_f
            rows = slice((j // 2 % 2) * WINDOW, (j // 2 % 2 + 1) * WINDOW)
            lanes = slice((j % 2) * 2 * WINDOW, (j % 2 + 1) * 2 * WINDOW)
            bias[1, j // 4, rows, lanes] = jnp.where(in_window, alibi, -jnp.inf)
            bias[0, j // 4, rows, lanes] = jnp.where(in_window & (kpos >= WINDOW), alibi, -jnp.inf)
        row = lax.broadcasted_iota(jnp.int32, (CHUNK, CHUNK), 0)
        col = lax.broadcasted_iota(jnp.int32, (CHUNK, CHUNK), 1)
        for hh in range(n_gmlp_heads):
            wcat[hh // 2, :, (hh % 2) * CHUNK:(hh % 2 + 1) * CHUNK] = (
                jnp.where(row >= col, ws_ref[hh], 0.0).astype(BF16))
        r = lax.broadcasted_iota(jnp.int32, (4 * WINDOW, LANES), 0)
        c = lax.broadcasted_iota(jnp.int32, (4 * WINDOW, LANES), 1)
        ones[...] = jnp.where((r < 2 * WINDOW) == (c < HEAD_DIM), 1.0, 0.0).astype(BF16)

    low_half = _low_half(WINDOW)
    has_prev0 = jnp.where(pl.program_id(1) == 0, 0, 1)
    zeros_kt = jnp.zeros((HEAD_DIM, 2 * WINDOW), BF16)

    def band(jb, rows_in_block, lanes):
        prev = (prev_ref[rows_in_block, lanes] if jb == 0 else
                act_ref[(jb - 1) * WINDOW + rows_in_block.start:
                        (jb - 1) * WINDOW + rows_in_block.stop, lanes])
        own = act_ref[jb * WINDOW + rows_in_block.start:jb * WINDOW + rows_in_block.stop, lanes]
        return [prev, own]

    all_rows = slice(0, WINDOW)

    def mix_block(jb):
        rows = slice(jb * WINDOW, (jb + 1) * WINDOW)
        for hk in range(N_KV_HEADS):
            feat_rows = slice((hk % 2) * HEAD_DIM, (hk % 2 + 1) * HEAD_DIM)
            kt_lanes = slice(lay.kt + (hk // 2) * LANES, lay.kt + (hk // 2 + 1) * LANES)
            kband = jnp.concatenate(band(jb, feat_rows, kt_lanes), axis=1)
            krhs = jnp.concatenate([jnp.concatenate([kband, zeros_kt], axis=1),
                                    jnp.concatenate([zeros_kt, kband], axis=1)], axis=0)
            vx_lanes = slice(lay.vx + hk * LANES, lay.vx + (hk + 1) * LANES)
            vy_lanes = slice(lay.vy + hk * LANES, lay.vy + (hk + 1) * LANES)
            vstack = jnp.concatenate(band(jb, all_rows, vx_lanes) + band(jb, all_rows, vy_lanes),
                                     axis=0)
            vrhs = jnp.concatenate([vstack, ones[...]], axis=1)
            cols = [slice((2 * hk + pr) * LANES, (2 * hk + pr + 1) * LANES) for pr in range(2)]
            q2 = jnp.concatenate([act_ref[rows, lay.q + cl.start:lay.q + cl.stop] for cl in cols],
                                 axis=0)
            s = jnp.dot(q2, krhs, preferred_element_type=F32)
            s = s + (bias[has_prev0, hk] if jb == 0 else bias[1, hk])
            probs, sink_terms = [], []
            for pr in range(2):
                sink_a = sinks_ref[layer, 4 * hk + 2 * pr] * LOG2_E
                sink_b = sinks_ref[layer, 4 * hk + 2 * pr + 1] * LOG2_E
                s_a = s[pr * WINDOW:(pr + 1) * WINDOW, :2 * WINDOW]
                s_b = s[pr * WINDOW:(pr + 1) * WINDOW, 2 * WINDOW:]
                m_a = jnp.maximum(jnp.max(s_a, axis=-1, keepdims=True), sink_a)
                m_b = jnp.maximum(jnp.max(s_b, axis=-1, keepdims=True), sink_b)
                probs.append(jnp.concatenate([jnp.exp2(s_a - m_a), jnp.exp2(s_b - m_b)],
                                             axis=1).astype(BF16))
                sink_terms.append(jnp.where(low_half, jnp.exp2(sink_a - m_a),
                                            jnp.exp2(sink_b - m_b)))
            o = jnp.dot(jnp.concatenate(probs, axis=0), vrhs,
                        preferred_element_type=F32)
            for pr in range(2):
                o_pr = o[pr * WINDOW:(pr + 1) * WINDOW, :]
                attn = o_pr[:, :LANES] / (o_pr[:, LANES:] + sink_terms[pr])
                ga = act_ref[rows, lay.ga + cols[pr].start:lay.ga + cols[pr].stop].astype(F32)
                mix[rows, cols[pr]] = (attn * ga).astype(BF16)

        for jp in range(n_gmlp_heads // 2):
            cl = slice(jp * LANES, (jp + 1) * LANES)
            vpair = act_ref[rows, lay.vs + cl.start:lay.vs + cl.stop]
            zero = jnp.zeros_like(vpair)
            rhs = jnp.concatenate([jnp.where(low_half, vpair, zero),
                                   jnp.where(low_half, zero, vpair)], axis=0)
            mixed = jnp.dot(wcat[jp], rhs, preferred_element_type=F32) + bfull_ref[:, cl]
            u = act_ref[rows, lay.u + cl.start:lay.u + cl.stop].astype(F32)
            gb = act_ref[rows, lay.gb + cl.start:lay.gb + cl.stop].astype(F32)
            mix[rows, d_attn + cl.start:d_attn + cl.stop] = (u * mixed * gb).astype(BF16)

    def out_rows(jo):
        rows = slice(jo * OUT_ROWS, (jo + 1) * OUT_ROWS)
        o_ref[rows, :] = x_ref[rows, :] + jnp.dot(mix[rows, :], wout_ref[...],
                                                  preferred_element_type=F32)

    blocks_per_out = OUT_ROWS // WINDOW
    for jb in range(n_blocks):
        mix_block(jb)
        if (jb + 1) % blocks_per_out == 0:
            out_rows(jb // blocks_per_out)


def _mixout(layer, sinks, x2d, act, w_s, b_full, w_out_b, *, batch, seq, d_attn, d_gmlp):
    t, d_model = x2d.shape
    d_kv = N_KV_HEADS * HEAD_DIM
    lay = _act_layout(d_attn, d_kv, d_gmlp)
    n_q_heads = d_attn // HEAD_DIM
    n_gmlp_heads = d_gmlp // HEAD_DIM
    tb = TOKENS_PER_STEP
    tiles = seq // tb
    blocks_per_tile = tb // WINDOW
    row = lambda b, i: (b * tiles + i, 0)
    prev = lambda b, i: (b * (seq // WINDOW) + jnp.maximum(i * blocks_per_tile - 1, 0), 0)
    return pl.pallas_call(
        functools.partial(_mixout_kernel, layer=layer, n_q_heads=n_q_heads,
                          n_gmlp_heads=n_gmlp_heads, tb=tb, d_attn=d_attn),
        grid=(batch, tiles),
        in_specs=[
            pl.BlockSpec(memory_space=pltpu.SMEM),
            pl.BlockSpec((tb, d_model), row),
            pl.BlockSpec((tb, lay.width), row),
            pl.BlockSpec((WINDOW, lay.band_width), prev),
            _resident((None, n_gmlp_heads, CHUNK, CHUNK), lambda b, i: (layer, 0, 0, 0)),
            _resident((None, CHUNK, d_gmlp), lambda b, i: (layer, 0, 0)),
            _resident((None, d_attn + d_gmlp, d_model), lambda b, i: (layer, 0, 0)),
        ],
        out_specs=pl.BlockSpec((tb, d_model), row),
        out_shape=jax.ShapeDtypeStruct((t, d_model), F32),
        scratch_shapes=[
            pltpu.VMEM((2, N_KV_HEADS, 2 * WINDOW, 4 * WINDOW), F32),
            pltpu.VMEM((n_gmlp_heads // 2, CHUNK, 2 * CHUNK), BF16),
            pltpu.VMEM((4 * WINDOW, LANES), BF16),
            pltpu.VMEM((tb, d_attn + d_gmlp), BF16),
        ],
        compiler_params=_compiler_params(2),
        name="mixout",
    )(sinks, x2d, act, act, w_s, b_full, w_out_b)


def kernel(x, norm_g, w_in, q_norm, k_norm, sinks, w_s, b_s, w_out):
    batch, seq, d_model = x.shape
    depth = norm_g.shape[0]
    n_q_heads = sinks.shape[1]
    n_gmlp_heads = w_s.shape[1]
    d_attn = n_q_heads * HEAD_DIM
    d_gmlp = n_gmlp_heads * HEAD_DIM
    d_kv = N_KV_HEADS * HEAD_DIM
    assert w_in.shape[2] == 2 * d_attn + 2 * d_kv + 3 * d_gmlp
    assert seq % TOKENS_PER_STEP == 0 and q_norm.shape[1] == HEAD_DIM
    assert n_q_heads == 4 * N_KV_HEADS and TOKENS_PER_STEP % OUT_ROWS == 0

    w_in_b = w_in.astype(BF16)
    w_out_b = w_out.astype(BF16)
    norm_g3 = norm_g.reshape(depth, 1, d_model)
    qn_tiled = jnp.tile(q_norm, (1, n_q_heads)).reshape(depth, 1, d_attn)
    kn_tiled = jnp.tile(k_norm, (1, N_KV_HEADS)).reshape(depth, 1, d_kv)
    b_full = jnp.repeat(jnp.swapaxes(b_s, 1, 2), HEAD_DIM, axis=2)

    x2d = x.reshape(batch * seq, d_model)
    for l in range(depth):
        act = _inproj(l, x2d, norm_g3, w_in_b, qn_tiled, kn_tiled,
                      d_attn=d_attn, d_kv=d_kv, d_gmlp=d_gmlp)
        x2d = _mixout(l, sinks, x2d, act, w_s, b_full, w_out_b,
                      batch=batch, seq=seq, d_attn=d_attn, d_gmlp=d_gmlp)
    return x2d.reshape(batch, seq, d_model)
```

```python
import collections
import functools

import jax
import jax.numpy as jnp
import numpy as np
from jax import lax
from jax.experimental import pallas as pl
from jax.experimental.pallas import tpu as pltpu

HEAD_DIM = 64
N_KV_HEADS = 4
WINDOW = 128
CHUNK = 128
RMS_EPS = 1e-6
LANES = 128
VMEM_LIMIT_BYTES = 56 * 1024 * 1024
TOKENS_PER_STEP = 512
OUT_ROWS = 256
SCORES_AHEAD = 2
LOG2_E = float(np.log2(np.e))

F32 = jnp.float32
BF16 = jnp.bfloat16

ActLayout = collections.namedtuple("ActLayout", "kt vx vy q ga u vs gb width band_width")


def _act_layout(d_attn, d_kv, d_gmlp):
    kt, vx = 0, d_kv
    vy = vx + 2 * d_kv
    q = vy + 2 * d_kv
    ga = q + d_attn
    u = ga + d_attn
    vs = u + d_gmlp
    gb = vs + d_gmlp
    return ActLayout(kt, vx, vy, q, ga, u, vs, gb, width=gb + d_gmlp, band_width=q)


def _compiler_params(n_grid_dims):
    return pltpu.CompilerParams(
        dimension_semantics=("arbitrary",) * n_grid_dims,
        vmem_limit_bytes=VMEM_LIMIT_BYTES,
    )


def _resident(block_shape, index_map):
    return pl.BlockSpec(block_shape, index_map, pipeline_mode=pl.Buffered(1))


def _low_half(rows):
    return lax.broadcasted_iota(jnp.int32, (rows, LANES), 1) < HEAD_DIM


def _head_rmsnorm(x, gain_tiled):
    rows, width = x.shape
    low_half = _low_half(rows)
    cols = []
    for c in range(width // LANES):
        blk = x[:, c * LANES:(c + 1) * LANES]
        sq = blk * blk
        ss_lo = jnp.sum(jnp.where(low_half, sq, 0.0), axis=-1, keepdims=True)
        ss_hi = jnp.sum(jnp.where(low_half, 0.0, sq), axis=-1, keepdims=True)
        r_lo = lax.rsqrt(ss_lo * (1.0 / HEAD_DIM) + RMS_EPS)
        r_hi = lax.rsqrt(ss_hi * (1.0 / HEAD_DIM) + RMS_EPS)
        cols.append(blk * jnp.where(low_half, r_lo, r_hi))
    return jnp.concatenate(cols, axis=-1) * gain_tiled


def _split_heads(x):
    rows, width = x.shape
    low_half = _low_half(rows)
    lo, hi = [], []
    for c in range(width // LANES):
        col = x[:, c * LANES:(c + 1) * LANES]
        swapped = pltpu.roll(col, HEAD_DIM, axis=1)
        lo += [jnp.where(low_half, col, 0.0), jnp.where(low_half, swapped, 0.0)]
        hi += [jnp.where(low_half, 0.0, swapped), jnp.where(low_half, 0.0, col)]
    return jnp.concatenate(lo, axis=-1), jnp.concatenate(hi, axis=-1)


def _gelu_exact(x):
    return 0.5 * x * (1.0 + lax.erf(x * np.float32(1.0 / np.sqrt(2.0))))


def _silu(x):
    half = 0.5 * x
    return half + half * jnp.tanh(half)


def _inproj_kernel(x_ref, g_ref, w_ref, qn_ref, kn_ref, act_ref, *, d_attn, d_kv, d_gmlp):
    lay = _act_layout(d_attn, d_kv, d_gmlp)
    x = x_ref[...]
    xg = (x * g_ref[...]).astype(BF16)
    r = lax.rsqrt(jnp.mean(x * x, axis=-1, keepdims=True) + RMS_EPS)

    def proj(lo, width):
        return r * jnp.dot(xg, w_ref[:, lo:lo + width], preferred_element_type=F32)

    def put(off, value):
        act_ref[:, off:off + value.shape[1]] = value.astype(BF16)

    q_off, k_off, v_off = 0, d_attn, d_attn + d_kv
    ga_off = v_off + d_kv
    u_off, vs_off, gb_off = ga_off + d_attn, ga_off + d_attn + d_gmlp, ga_off + d_attn + 2 * d_gmlp

    q = _head_rmsnorm(proj(q_off, d_attn), qn_ref[...])
    put(lay.q, q * (HEAD_DIM ** -0.5 * LOG2_E))
    k = _head_rmsnorm(proj(k_off, d_kv), kn_ref[...])
    for jb in range(k.shape[0] // WINDOW):
        for c in range(d_kv // LANES):
            tile = k[jb * WINDOW:(jb + 1) * WINDOW, c * LANES:(c + 1) * LANES]
            act_ref[jb * WINDOW:(jb + 1) * WINDOW,
                    lay.kt + c * LANES:lay.kt + (c + 1) * LANES] = tile.T.astype(BF16)
    vx, vy = _split_heads(proj(v_off, d_kv))
    put(lay.vx, vx)
    put(lay.vy, vy)
    put(lay.u, _gelu_exact(proj(u_off, d_gmlp)))
    put(lay.vs, _gelu_exact(proj(vs_off, d_gmlp)))
    put(lay.ga, _silu(proj(ga_off, d_attn)))
    put(lay.gb, _silu(proj(gb_off, d_gmlp)))


def _inproj(layer, x2d, norm_g, w_in_b, qn_tiled, kn_tiled, *, d_attn, d_kv, d_gmlp):
    t, d_model = x2d.shape
    d_in = w_in_b.shape[2]
    tm = TOKENS_PER_STEP
    width = _act_layout(d_attn, d_kv, d_gmlp).width
    row = lambda i: (i, 0)
    const = lambda i: (layer, 0, 0)
    return pl.pallas_call(
        functools.partial(_inproj_kernel, d_attn=d_attn, d_kv=d_kv, d_gmlp=d_gmlp),
        grid=(t // tm,),
        in_specs=[
            pl.BlockSpec((tm, d_model), row),
            _resident((None, 1, d_model), const),
            _resident((None, d_model, d_in), const),
            _resident((None, 1, d_attn), const),
            _resident((None, 1, d_kv), const),
        ],
        out_specs=pl.BlockSpec((tm, width), row),
        out_shape=jax.ShapeDtypeStruct((t, width), BF16),
        compiler_params=_compiler_params(1),
        name="inproj",
    )(x2d, norm_g, w_in_b, qn_tiled, kn_tiled)


def _mixout_kernel(sinks_ref, x_ref, act_ref, prev_ref, ws_ref, bfull_ref, wout_ref,
                   o_ref,
                   bias, wcat, ones, mix,
                   *, layer, n_q_heads, n_gmlp_heads, tb, d_attn):
    n_blocks = tb // WINDOW
    d_kv = N_KV_HEADS * HEAD_DIM
    d_gmlp = n_gmlp_heads * HEAD_DIM
    lay = _act_layout(d_attn, d_kv, d_gmlp)
    first_tile = (pl.program_id(0) == 0) & (pl.program_id(1) == 0)

    @pl.when(first_tile)
    def _init_tables():
        qpos = lax.broadcasted_iota(jnp.int32, (WINDOW, 2 * WINDOW), 0) + WINDOW
        kpos = lax.broadcasted_iota(jnp.int32, (WINDOW, 2 * WINDOW), 1)
        dist = qpos - kpos
        in_window = (dist >= 0) & (dist < WINDOW)
        dist_f = dist.astype(F32)
        for j in range(n_q_heads):
            slope = np.float32(2.0 ** (-8.0 * (j + 1) / n_q_heads))
            alibi = (-slope * LOG2_E) * dist_f
            rows = slice((j // 2 % 2) * WINDOW, (j // 2 % 2 + 1) * WINDOW)
            lanes = slice((j % 2) * 2 * WINDOW, (j % 2 + 1) * 2 * WINDOW)
            bias[1, j // 4, rows, lanes] = jnp.where(in_window, alibi, -jnp.inf)
            bias[0, j // 4, rows, lanes] = jnp.where(in_window & (kpos >= WINDOW), alibi, -jnp.inf)
        row = lax.broadcasted_iota(jnp.int32, (CHUNK, CHUNK), 0)
        col = lax.broadcasted_iota(jnp.int32, (CHUNK, CHUNK), 1)
        for hh in range(n_gmlp_heads):
            wcat[hh // 2, :, (hh % 2) * CHUNK:(hh % 2 + 1) * CHUNK] = (
                jnp.where(row >= col, ws_ref[hh], 0.0).astype(BF16))
        r = lax.broadcasted_iota(jnp.int32, (4 * WINDOW, LANES), 0)
        c = lax.broadcasted_iota(jnp.int32, (4 * WINDOW, LANES), 1)
        ones[...] = jnp.where((r < 2 * WINDOW) == (c < HEAD_DIM), 1.0, 0.0).astype(BF16)

    low_half = _low_half(WINDOW)
    has_prev0 = jnp.where(pl.program_id(1) == 0, 0, 1)
    zeros_kt = jnp.zeros((HEAD_DIM, 2 * WINDOW), BF16)

    def band(jb, rows_in_block, lanes):
        prev = (prev_ref[rows_in_block, lanes] if jb == 0 else
                act_ref[(jb - 1) * WINDOW + rows_in_block.start:
                        (jb - 1) * WINDOW + rows_in_block.stop, lanes])
        own = act_ref[jb * WINDOW + rows_in_block.start:jb * WINDOW + rows_in_block.stop, lanes]
        return [prev, own]

    all_rows = slice(0, WINDOW)

    def block_rows(jb):
        return slice(jb * WINDOW, (jb + 1) * WINDOW)

    def q_cols(hk):
        return [slice((2 * hk + pr) * LANES, (2 * hk + pr + 1) * LANES) for pr in range(2)]

    def scores(jb, hk):
        feat_rows = slice((hk % 2) * HEAD_DIM, (hk % 2 + 1) * HEAD_DIM)
        kt_lanes = slice(lay.kt + (hk // 2) * LANES, lay.kt + (hk // 2 + 1) * LANES)
        kband = jnp.concatenate(band(jb, feat_rows, kt_lanes), axis=1)
        krhs = jnp.concatenate([jnp.concatenate([kband, zeros_kt], axis=1),
                                jnp.concatenate([zeros_kt, kband], axis=1)], axis=0)
        q2 = jnp.concatenate([act_ref[block_rows(jb), lay.q + cl.start:lay.q + cl.stop]
                              for cl in q_cols(hk)], axis=0)
        s = jnp.dot(q2, krhs, preferred_element_type=F32)
        return s + (bias[has_prev0, hk] if jb == 0 else bias[1, hk])

    def softmax_numerators(hk, s):
        probs, sink_terms = [], []
        for pr in range(2):
            sink_a = sinks_ref[layer, 4 * hk + 2 * pr] * LOG2_E
            sink_b = sinks_ref[layer, 4 * hk + 2 * pr + 1] * LOG2_E
            s_a = s[pr * WINDOW:(pr + 1) * WINDOW, :2 * WINDOW]
            s_b = s[pr * WINDOW:(pr + 1) * WINDOW, 2 * WINDOW:]
            m_a = jnp.maximum(jnp.max(s_a, axis=-1, keepdims=True), sink_a)
            m_b = jnp.maximum(jnp.max(s_b, axis=-1, keepdims=True), sink_b)
            probs.append(jnp.concatenate([jnp.exp2(s_a - m_a), jnp.exp2(s_b - m_b)],
                                         axis=1).astype(BF16))
            sink_terms.append(jnp.where(low_half, jnp.exp2(sink_a - m_a),
                                        jnp.exp2(sink_b - m_b)))
        return jnp.concatenate(probs, axis=0), sink_terms

    def weighted_values(jb, hk, probs):
        vx_lanes = slice(lay.vx + hk * LANES, lay.vx + (hk + 1) * LANES)
        vy_lanes = slice(lay.vy + hk * LANES, lay.vy + (hk + 1) * LANES)
        vstack = jnp.concatenate(band(jb, all_rows, vx_lanes) + band(jb, all_rows, vy_lanes),
                                 axis=0)
        vrhs = jnp.concatenate([vstack, ones[...]], axis=1)
        return jnp.dot(probs, vrhs, preferred_element_type=F32)

    def normalize_and_gate(jb, hk, o, sink_terms):
        rows = block_rows(jb)
        for pr, cl in enumerate(q_cols(hk)):
            o_pr = o[pr * WINDOW:(pr + 1) * WINDOW, :]
            attn = o_pr[:, :LANES] / (o_pr[:, LANES:] + sink_terms[pr])
            ga = act_ref[rows, lay.ga + cl.start:lay.ga + cl.stop]
            mix[rows, cl] = attn.astype(BF16) * ga

    def spatial_gating_pair(jb, jp):
        rows = block_rows(jb)
        cl = slice(jp * LANES, (jp + 1) * LANES)
        vpair = act_ref[rows, lay.vs + cl.start:lay.vs + cl.stop]
        zero = jnp.zeros_like(vpair)
        rhs = jnp.concatenate([jnp.where(low_half, vpair, zero),
                               jnp.where(low_half, zero, vpair)], axis=0)
        mixed = jnp.dot(wcat[jp], rhs, preferred_element_type=F32) + bfull_ref[:, cl]
        u = act_ref[rows, lay.u + cl.start:lay.u + cl.stop]
        gb = act_ref[rows, lay.gb + cl.start:lay.gb + cl.stop]
        mix[rows, d_attn + cl.start:d_attn + cl.stop] = mixed.astype(BF16) * (u * gb)

    def out_rows(jo):
        rows = slice(jo * OUT_ROWS, (jo + 1) * OUT_ROWS)
        o_ref[rows, :] = x_ref[rows, :] + jnp.dot(mix[rows, :], wout_ref[...],
                                                  preferred_element_type=F32)

    blocks_per_out = OUT_ROWS // WINDOW
    pairs_per_head = n_gmlp_heads // 2 // N_KV_HEADS
    for jo in range(n_blocks // blocks_per_out):
        items = [(jb, hk) for jb in range(jo * blocks_per_out, (jo + 1) * blocks_per_out)
                 for hk in range(N_KV_HEADS)]
        s_queue = [scores(*item) for item in items[:SCORES_AHEAD]]
        pending = None
        for idx, (jb, hk) in enumerate(items):
            if idx + SCORES_AHEAD < len(items):
                s_queue.append(scores(*items[idx + SCORES_AHEAD]))
            probs, sink_terms = softmax_numerators(hk, s_queue.pop(0))
            o = weighted_values(jb, hk, probs)
            if pending is not None:
                normalize_and_gate(*pending)
            pending = (jb, hk, o, sink_terms)
            for jp in range(hk * pairs_per_head, (hk + 1) * pairs_per_head):
                spatial_gating_pair(jb, jp)
        normalize_and_gate(*pending)
        out_rows(jo)


def _mixout(layer, sinks, x2d, act, w_s, b_full, w_out_b, *, batch, seq, d_attn, d_gmlp):
    t, d_model = x2d.shape
    d_kv = N_KV_HEADS * HEAD_DIM
    lay = _act_layout(d_attn, d_kv, d_gmlp)
    n_q_heads = d_attn // HEAD_DIM
    n_gmlp_heads = d_gmlp // HEAD_DIM
    tb = TOKENS_PER_STEP
    tiles = seq // tb
    blocks_per_tile = tb // WINDOW
    row = lambda b, i: (b * tiles + i, 0)
    prev = lambda b, i: (b * (seq // WINDOW) + jnp.maximum(i * blocks_per_tile - 1, 0), 0)
    return pl.pallas_call(
        functools.partial(_mixout_kernel, layer=layer, n_q_heads=n_q_heads,
                          n_gmlp_heads=n_gmlp_heads, tb=tb, d_attn=d_attn),
        grid=(batch, tiles),
        in_specs=[
            pl.BlockSpec(memory_space=pltpu.SMEM),
            pl.BlockSpec((tb, d_model), row),
            pl.BlockSpec((tb, lay.width), row),
            pl.BlockSpec((WINDOW, lay.band_width), prev),
            _resident((None, n_gmlp_heads, CHUNK, CHUNK), lambda b, i: (layer, 0, 0, 0)),
            _resident((None, CHUNK, d_gmlp), lambda b, i: (layer, 0, 0)),
            _resident((None, d_attn + d_gmlp, d_model), lambda b, i: (layer, 0, 0)),
        ],
        out_specs=pl.BlockSpec((tb, d_model), row),
        out_shape=jax.ShapeDtypeStruct((t, d_model), F32),
        scratch_shapes=[
            pltpu.VMEM((2, N_KV_HEADS, 2 * WINDOW, 4 * WINDOW), F32),
            pltpu.VMEM((n_gmlp_heads // 2, CHUNK, 2 * CHUNK), BF16),
            pltpu.VMEM((4 * WINDOW, LANES), BF16),
            pltpu.VMEM((tb, d_attn + d_gmlp), BF16),
        ],
        compiler_params=_compiler_params(2),
        name="mixout",
    )(sinks, x2d, act, act, w_s, b_full, w_out_b)


def kernel(x, norm_g, w_in, q_norm, k_norm, sinks, w_s, b_s, w_out):
    batch, seq, d_model = x.shape
    depth = norm_g.shape[0]
    n_q_heads = sinks.shape[1]
    n_gmlp_heads = w_s.shape[1]
    d_attn = n_q_heads * HEAD_DIM
    d_gmlp = n_gmlp_heads * HEAD_DIM
    d_kv = N_KV_HEADS * HEAD_DIM
    assert w_in.shape[2] == 2 * d_attn + 2 * d_kv + 3 * d_gmlp
    assert seq % TOKENS_PER_STEP == 0 and q_norm.shape[1] == HEAD_DIM
    assert n_q_heads == 4 * N_KV_HEADS and TOKENS_PER_STEP % OUT_ROWS == 0

    w_in_b = w_in.astype(BF16)
    w_out_b = w_out.astype(BF16)
    norm_g3 = norm_g.reshape(depth, 1, d_model)
    qn_tiled = jnp.tile(q_norm, (1, n_q_heads)).reshape(depth, 1, d_attn)
    kn_tiled = jnp.tile(k_norm, (1, N_KV_HEADS)).reshape(depth, 1, d_kv)
    b_full = jnp.repeat(jnp.swapaxes(b_s, 1, 2), HEAD_DIM, axis=2)

    x2d = x.reshape(batch * seq, d_model)
    for l in range(depth):
        act = _inproj(l, x2d, norm_g3, w_in_b, qn_tiled, kn_tiled,
                      d_attn=d_attn, d_kv=d_kv, d_gmlp=d_gmlp)
        x2d = _mixout(l, sinks, x2d, act, w_s, b_full, w_out_b,
                      batch=batch, seq=seq, d_attn=d_attn, d_gmlp=d_gmlp)
    return x2d.reshape(batch, seq, d_model)
```

```python
import collections
import functools

import jax
import jax.numpy as jnp
import numpy as np
from jax import lax
from jax.experimental import pallas as pl
from jax.experimental.pallas import tpu as pltpu

HEAD_DIM = 64
N_KV_HEADS = 4
WINDOW = 128
CHUNK = 128
RMS_EPS = 1e-6
LANES = 128
VMEM_LIMIT_BYTES = 56 * 1024 * 1024
TOKENS_PER_STEP = 512
WEIGHT_CHUNK_COLS = 256
OUT_ROWS = 512
SCORES_AHEAD = 2
LOG2_E = float(np.log2(np.e))

F32 = jnp.float32
BF16 = jnp.bfloat16

ActLayout = collections.namedtuple("ActLayout", "kt vx vy q ga u vs gb width band_width")


def _act_layout(d_attn, d_kv, d_gmlp):
    kt, vx = 0, d_kv
    vy = vx + 2 * d_kv
    q = vy + 2 * d_kv
    ga = q + d_attn
    u = ga + d_attn
    vs = u + d_gmlp
    gb = vs + d_gmlp
    return ActLayout(kt, vx, vy, q, ga, u, vs, gb, width=gb + d_gmlp, band_width=q)


def _compiler_params(n_grid_dims):
    return pltpu.CompilerParams(
        dimension_semantics=("arbitrary",) * n_grid_dims,
        vmem_limit_bytes=VMEM_LIMIT_BYTES,
    )


def _resident(block_shape, index_map):
    return pl.BlockSpec(block_shape, index_map, pipeline_mode=pl.Buffered(1))


def _load_weight_as_bf16(w_hbm, layer, w_vmem, stage, sem):
    chunk = stage.shape[2]
    n_chunks = w_vmem.shape[1] // chunk

    def copy(c):
        return pltpu.make_async_copy(w_hbm.at[layer, :, c * chunk:(c + 1) * chunk],
                                     stage.at[c % 2], sem.at[c % 2])

    copy(0).start()
    for c in range(n_chunks):
        if c + 1 < n_chunks:
            copy(c + 1).start()
        copy(c).wait()
        w_vmem[:, c * chunk:(c + 1) * chunk] = stage[c % 2].astype(BF16)


def _weight_scratch(rows, cols):
    return [pltpu.VMEM((rows, cols), BF16),
            pltpu.VMEM((2, rows, WEIGHT_CHUNK_COLS), F32),
            pltpu.SemaphoreType.DMA((2,))]


def _low_half(rows):
    return lax.broadcasted_iota(jnp.int32, (rows, LANES), 1) < HEAD_DIM


def _head_rmsnorm(x, gain_tiled):
    rows, width = x.shape
    low_half = _low_half(rows)
    cols = []
    for c in range(width // LANES):
        blk = x[:, c * LANES:(c + 1) * LANES]
        sq = blk * blk
        ss_lo = jnp.sum(jnp.where(low_half, sq, 0.0), axis=-1, keepdims=True)
        ss_hi = jnp.sum(jnp.where(low_half, 0.0, sq), axis=-1, keepdims=True)
        r_lo = lax.rsqrt(ss_lo * (1.0 / HEAD_DIM) + RMS_EPS)
        r_hi = lax.rsqrt(ss_hi * (1.0 / HEAD_DIM) + RMS_EPS)
        cols.append(blk * jnp.where(low_half, r_lo, r_hi))
    return jnp.concatenate(cols, axis=-1) * gain_tiled


def _split_heads(x):
    rows, width = x.shape
    low_half = _low_half(rows)
    lo, hi = [], []
    for c in range(width // LANES):
        col = x[:, c * LANES:(c + 1) * LANES]
        swapped = pltpu.roll(col, HEAD_DIM, axis=1)
        lo += [jnp.where(low_half, col, 0.0), jnp.where(low_half, swapped, 0.0)]
        hi += [jnp.where(low_half, 0.0, swapped), jnp.where(low_half, 0.0, col)]
    return jnp.concatenate(lo, axis=-1), jnp.concatenate(hi, axis=-1)


def _gelu_exact(x):
    return 0.5 * x * (1.0 + lax.erf(x * np.float32(1.0 / np.sqrt(2.0))))


def _silu(x):
    half = 0.5 * x
    return half + half * jnp.tanh(half)


def _inproj_kernel(x_ref, g_ref, w_hbm_ref, qn_ref, kn_ref, act_ref, w_ref, stage, sem,
                   *, layer, d_attn, d_kv, d_gmlp):
    lay = _act_layout(d_attn, d_kv, d_gmlp)

    @pl.when(pl.program_id(0) == 0)
    def _load_weights():
        _load_weight_as_bf16(w_hbm_ref, layer, w_ref, stage, sem)

    x = x_ref[...]
    xg = (x * g_ref[...]).astype(BF16)
    r = lax.rsqrt(jnp.mean(x * x, axis=-1, keepdims=True) + RMS_EPS)

    def proj(lo, width):
        return r * jnp.dot(xg, w_ref[:, lo:lo + width], preferred_element_type=F32)

    def put(off, value):
        act_ref[:, off:off + value.shape[1]] = value.astype(BF16)

    q_off, k_off, v_off = 0, d_attn, d_attn + d_kv
    ga_off = v_off + d_kv
    u_off, vs_off, gb_off = ga_off + d_attn, ga_off + d_attn + d_gmlp, ga_off + d_attn + 2 * d_gmlp

    q = _head_rmsnorm(proj(q_off, d_attn), qn_ref[...])
    put(lay.q, q * (HEAD_DIM ** -0.5 * LOG2_E))
    k = _head_rmsnorm(proj(k_off, d_kv), kn_ref[...])
    for jb in range(k.shape[0] // WINDOW):
        for c in range(d_kv // LANES):
            tile = k[jb * WINDOW:(jb + 1) * WINDOW, c * LANES:(c + 1) * LANES]
            act_ref[jb * WINDOW:(jb + 1) * WINDOW,
                    lay.kt + c * LANES:lay.kt + (c + 1) * LANES] = tile.T.astype(BF16)
    vx, vy = _split_heads(proj(v_off, d_kv))
    put(lay.vx, vx)
    put(lay.vy, vy)
    put(lay.u, _gelu_exact(proj(u_off, d_gmlp)))
    put(lay.vs, _gelu_exact(proj(vs_off, d_gmlp)))
    put(lay.ga, _silu(proj(ga_off, d_attn)))
    put(lay.gb, _silu(proj(gb_off, d_gmlp)))


def _inproj(layer, x2d, norm_g, w_in, qn_tiled, kn_tiled, *, d_attn, d_kv, d_gmlp):
    t, d_model = x2d.shape
    d_in = w_in.shape[2]
    tm = TOKENS_PER_STEP
    width = _act_layout(d_attn, d_kv, d_gmlp).width
    row = lambda i: (i, 0)
    const = lambda i: (layer, 0, 0)
    return pl.pallas_call(
        functools.partial(_inproj_kernel, layer=layer, d_attn=d_attn, d_kv=d_kv, d_gmlp=d_gmlp),
        grid=(t // tm,),
        in_specs=[
            pl.BlockSpec((tm, d_model), row),
            _resident((None, 1, d_model), const),
            pl.BlockSpec(memory_space=pl.ANY),
            _resident((None, 1, d_attn), const),
            _resident((None, 1, d_kv), const),
        ],
        out_specs=pl.BlockSpec((tm, width), row),
        out_shape=jax.ShapeDtypeStruct((t, width), BF16),
        scratch_shapes=_weight_scratch(d_model, d_in),
        compiler_params=_compiler_params(1),
        name="inproj",
    )(x2d, norm_g, w_in, qn_tiled, kn_tiled)


def _mixout_kernel(sinks_ref, x_ref, act_ref, prev_ref, ws_ref, bfull_ref, wout_hbm_ref,
                   o_ref,
                   bias, wcat, ones, mix, wout_ref, stage, sem,
                   *, layer, n_q_heads, n_gmlp_heads, tb, d_attn):
    n_blocks = tb // WINDOW
    d_kv = N_KV_HEADS * HEAD_DIM
    d_gmlp = n_gmlp_heads * HEAD_DIM
    lay = _act_layout(d_attn, d_kv, d_gmlp)
    first_tile = (pl.program_id(0) == 0) & (pl.program_id(1) == 0)

    @pl.when(first_tile)
    def _init_tables():
        _load_weight_as_bf16(wout_hbm_ref, layer, wout_ref, stage, sem)
        qpos = lax.broadcasted_iota(jnp.int32, (WINDOW, 2 * WINDOW), 0) + WINDOW
        kpos = lax.broadcasted_iota(jnp.int32, (WINDOW, 2 * WINDOW), 1)
        dist = qpos - kpos
        in_window = (dist >= 0) & (dist < WINDOW)
        dist_f = dist.astype(F32)
        for j in range(n_q_heads):
            slope = np.float32(2.0 ** (-8.0 * (j + 1) / n_q_heads))
            alibi = (-slope * LOG2_E) * dist_f
            rows = slice((j // 2 % 2) * WINDOW, (j // 2 % 2 + 1) * WINDOW)
            lanes = slice((j % 2) * 2 * WINDOW, (j % 2 + 1) * 2 * WINDOW)
            bias[1, j // 4, rows, lanes] = jnp.where(in_window, alibi, -jnp.inf)
            bias[0, j // 4, rows, lanes] = jnp.where(in_window & (kpos >= WINDOW), alibi, -jnp.inf)
        row = lax.broadcasted_iota(jnp.int32, (CHUNK, CHUNK), 0)
        col = lax.broadcasted_iota(jnp.int32, (CHUNK, CHUNK), 1)
        for hh in range(n_gmlp_heads):
            wcat[hh // 2, :, (hh % 2) * CHUNK:(hh % 2 + 1) * CHUNK] = (
                jnp.where(row >= col, ws_ref[hh], 0.0).astype(BF16))
        r = lax.broadcasted_iota(jnp.int32, (4 * WINDOW, LANES), 0)
        c = lax.broadcasted_iota(jnp.int32, (4 * WINDOW, LANES), 1)
        ones[...] = jnp.where((r < 2 * WINDOW) == (c < HEAD_DIM), 1.0, 0.0).astype(BF16)

    low_half = _low_half(WINDOW)
    has_prev0 = jnp.where(pl.program_id(1) == 0, 0, 1)
    zeros_kt = jnp.zeros((HEAD_DIM, 2 * WINDOW), BF16)

    def band(jb, rows_in_block, lanes):
        prev = (prev_ref[rows_in_block, lanes] if jb == 0 else
                act_ref[(jb - 1) * WINDOW + rows_in_block.start:
                        (jb - 1) * WINDOW + rows_in_block.stop, lanes])
        own = act_ref[jb * WINDOW + rows_in_block.start:jb * WINDOW + rows_in_block.stop, lanes]
        return [prev, own]

    all_rows = slice(0, WINDOW)

    def block_rows(jb):
        return slice(jb * WINDOW, (jb + 1) * WINDOW)

    def q_cols(hk):
        return [slice((2 * hk + pr) * LANES, (2 * hk + pr + 1) * LANES) for pr in range(2)]

    def scores(jb, hk):
        feat_rows = slice((hk % 2) * HEAD_DIM, (hk % 2 + 1) * HEAD_DIM)
        kt_lanes = slice(lay.kt + (hk // 2) * LANES, lay.kt + (hk // 2 + 1) * LANES)
        kband = jnp.concatenate(band(jb, feat_rows, kt_lanes), axis=1)
        krhs = jnp.concatenate([jnp.concatenate([kband, zeros_kt], axis=1),
                                jnp.concatenate([zeros_kt, kband], axis=1)], axis=0)
        q2 = jnp.concatenate([act_ref[block_rows(jb), lay.q + cl.start:lay.q + cl.stop]
                              for cl in q_cols(hk)], axis=0)
        s = jnp.dot(q2, krhs, preferred_element_type=F32)
        return s + (bias[has_prev0, hk] if jb == 0 else bias[1, hk])

    def softmax_numerators(hk, s):
        probs, sink_terms = [], []
        for pr in range(2):
            sink_a = sinks_ref[layer, 4 * hk + 2 * pr] * LOG2_E
            sink_b = sinks_ref[layer, 4 * hk + 2 * pr + 1] * LOG2_E
            s_a = s[pr * WINDOW:(pr + 1) * WINDOW, :2 * WINDOW]
            s_b = s[pr * WINDOW:(pr + 1) * WINDOW, 2 * WINDOW:]
            m_a = jnp.maximum(jnp.max(s_a, axis=-1, keepdims=True), sink_a)
            m_b = jnp.maximum(jnp.max(s_b, axis=-1, keepdims=True), sink_b)
            probs.append(jnp.concatenate([jnp.exp2(s_a - m_a), jnp.exp2(s_b - m_b)],
                                         axis=1).astype(BF16))
            sink_terms.append(jnp.where(low_half, jnp.exp2(sink_a - m_a),
                                        jnp.exp2(sink_b - m_b)))
        return jnp.concatenate(probs, axis=0), sink_terms

    def weighted_values(jb, hk, probs):
        vx_lanes = slice(lay.vx + hk * LANES, lay.vx + (hk + 1) * LANES)
        vy_lanes = slice(lay.vy + hk * LANES, lay.vy + (hk + 1) * LANES)
        vstack = jnp.concatenate(band(jb, all_rows, vx_lanes) + band(jb, all_rows, vy_lanes),
                                 axis=0)
        vrhs = jnp.concatenate([vstack, ones[...]], axis=1)
        return jnp.dot(probs, vrhs, preferred_element_type=F32)

    def normalize_and_gate(jb, hk, o, sink_terms):
        rows = block_rows(jb)
        for pr, cl in enumerate(q_cols(hk)):
            o_pr = o[pr * WINDOW:(pr + 1) * WINDOW, :]
            attn = o_pr[:, :LANES] / (o_pr[:, LANES:] + sink_terms[pr])
            ga = act_ref[rows, lay.ga + cl.start:lay.ga + cl.stop]
            mix[rows, cl] = attn.astype(BF16) * ga

    def spatial_gating_pair(jb, jp):
        rows = block_rows(jb)
        cl = slice(jp * LANES, (jp + 1) * LANES)
        vpair = act_ref[rows, lay.vs + cl.start:lay.vs + cl.stop]
        zero = jnp.zeros_like(vpair)
        rhs = jnp.concatenate([jnp.where(low_half, vpair, zero),
                               jnp.where(low_half, zero, vpair)], axis=0)
        mixed = jnp.dot(wcat[jp], rhs, preferred_element_type=F32) + bfull_ref[:, cl]
        u = act_ref[rows, lay.u + cl.start:lay.u + cl.stop]
        gb = act_ref[rows, lay.gb + cl.start:lay.gb + cl.stop]
        mix[rows, d_attn + cl.start:d_attn + cl.stop] = mixed.astype(BF16) * (u * gb)

    def out_rows(jo):
        rows = slice(jo * OUT_ROWS, (jo + 1) * OUT_ROWS)
        o_ref[rows, :] = x_ref[rows, :] + jnp.dot(mix[rows, :], wout_ref[...],
                                                  preferred_element_type=F32)

    blocks_per_out = OUT_ROWS // WINDOW
    pairs_per_head = n_gmlp_heads // 2 // N_KV_HEADS
    for jo in range(n_blocks // blocks_per_out):
        items = [(jb, hk) for jb in range(jo * blocks_per_out, (jo + 1) * blocks_per_out)
                 for hk in range(N_KV_HEADS)]
        s_queue = [scores(*item) for item in items[:SCORES_AHEAD]]
        pending = None
        for idx, (jb, hk) in enumerate(items):
            if idx + SCORES_AHEAD < len(items):
                s_queue.append(scores(*items[idx + SCORES_AHEAD]))
            probs, sink_terms = softmax_numerators(hk, s_queue.pop(0))
            o = weighted_values(jb, hk, probs)
            if pending is not None:
                normalize_and_gate(*pending)
            pending = (jb, hk, o, sink_terms)
            for jp in range(hk * pairs_per_head, (hk + 1) * pairs_per_head):
                spatial_gating_pair(jb, jp)
        normalize_and_gate(*pending)
        out_rows(jo)


def _mixout(layer, sinks, x2d, act, w_s, b_full, w_out, *, batch, seq, d_attn, d_gmlp):
    t, d_model = x2d.shape
    d_kv = N_KV_HEADS * HEAD_DIM
    lay = _act_layout(d_attn, d_kv, d_gmlp)
    n_q_heads = d_attn // HEAD_DIM
    n_gmlp_heads = d_gmlp // HEAD_DIM
    tb = TOKENS_PER_STEP
    tiles = seq // tb
    blocks_per_tile = tb // WINDOW
    row = lambda b, i: (b * tiles + i, 0)
    prev = lambda b, i: (b * (seq // WINDOW) + jnp.maximum(i * blocks_per_tile - 1, 0), 0)
    return pl.pallas_call(
        functools.partial(_mixout_kernel, layer=layer, n_q_heads=n_q_heads,
                          n_gmlp_heads=n_gmlp_heads, tb=tb, d_attn=d_attn),
        grid=(batch, tiles),
        in_specs=[
            pl.BlockSpec(memory_space=pltpu.SMEM),
            pl.BlockSpec((tb, d_model), row),
            pl.BlockSpec((tb, lay.width), row),
            pl.BlockSpec((WINDOW, lay.band_width), prev),
            _resident((None, n_gmlp_heads, CHUNK, CHUNK), lambda b, i: (layer, 0, 0, 0)),
            _resident((None, CHUNK, d_gmlp), lambda b, i: (layer, 0, 0)),
            pl.BlockSpec(memory_space=pl.ANY),
        ],
        out_specs=pl.BlockSpec((tb, d_model), row),
        out_shape=jax.ShapeDtypeStruct((t, d_model), F32),
        scratch_shapes=[
            pltpu.VMEM((2, N_KV_HEADS, 2 * WINDOW, 4 * WINDOW), F32),
            pltpu.VMEM((n_gmlp_heads // 2, CHUNK, 2 * CHUNK), BF16),
            pltpu.VMEM((4 * WINDOW, LANES), BF16),
            pltpu.VMEM((tb, d_attn + d_gmlp), BF16),
        ] + _weight_scratch(d_attn + d_gmlp, d_model),
        compiler_params=_compiler_params(2),
        name="mixout",
    )(sinks, x2d, act, act, w_s, b_full, w_out)


def kernel(x, norm_g, w_in, q_norm, k_norm, sinks, w_s, b_s, w_out):
    batch, seq, d_model = x.shape
    depth = norm_g.shape[0]
    n_q_heads = sinks.shape[1]
    n_gmlp_heads = w_s.shape[1]
    d_attn = n_q_heads * HEAD_DIM
    d_gmlp = n_gmlp_heads * HEAD_DIM
    d_kv = N_KV_HEADS * HEAD_DIM
    assert w_in.shape[2] == 2 * d_attn + 2 * d_kv + 3 * d_gmlp
    assert seq % TOKENS_PER_STEP == 0 and q_norm.shape[1] == HEAD_DIM
    assert n_q_heads == 4 * N_KV_HEADS and TOKENS_PER_STEP % OUT_ROWS == 0

    norm_g3 = norm_g.reshape(depth, 1, d_model)
    qn_tiled = jnp.tile(q_norm, (1, n_q_heads)).reshape(depth, 1, d_attn)
    kn_tiled = jnp.tile(k_norm, (1, N_KV_HEADS)).reshape(depth, 1, d_kv)
    b_full = jnp.repeat(jnp.swapaxes(b_s, 1, 2), HEAD_DIM, axis=2)

    x2d = x.reshape(batch * seq, d_model)
    for l in range(depth):
        act = _inproj(l, x2d, norm_g3, w_in, qn_tiled, kn_tiled,
                      d_attn=d_attn, d_kv=d_kv, d_gmlp=d_gmlp)
        x2d = _mixout(l, sinks, x2d, act, w_s, b_full, w_out,
                      batch=batch, seq=seq, d_attn=d_attn, d_gmlp=d_gmlp)
    return x2d.reshape(batch, seq, d_model)
```

```python
import collections
import functools

import jax
import jax.numpy as jnp
import numpy as np
from jax import lax
from jax.experimental import pallas as pl
from jax.experimental.pallas import tpu as pltpu

HEAD_DIM = 64
N_KV_HEADS = 4
WINDOW = 128
CHUNK = 128
RMS_EPS = 1e-6
LANES = 128
VMEM_LIMIT_BYTES = 56 * 1024 * 1024
TOKENS_PER_STEP = 512
OUT_ROWS = 512
SCORES_AHEAD = 2
LOG2_E = float(np.log2(np.e))

F32 = jnp.float32
BF16 = jnp.bfloat16

ActLayout = collections.namedtuple("ActLayout", "kt vx vy q ga u vs gb width band_width")


def _act_layout(d_attn, d_kv, d_gmlp):
    kt, vx = 0, d_kv
    vy = vx + 2 * d_kv
    q = vy + 2 * d_kv
    ga = q + d_attn
    u = ga + d_attn
    vs = u + d_gmlp
    gb = vs + d_gmlp
    return ActLayout(kt, vx, vy, q, ga, u, vs, gb, width=gb + d_gmlp, band_width=q)


def _compiler_params(n_grid_dims):
    return pltpu.CompilerParams(
        dimension_semantics=("arbitrary",) * n_grid_dims,
        vmem_limit_bytes=VMEM_LIMIT_BYTES,
    )


def _resident(block_shape, index_map):
    return pl.BlockSpec(block_shape, index_map, pipeline_mode=pl.Buffered(1))


def _cast_slice_specs(weight, layer, n_steps, step_index):
    _, rows, cols = weight.shape
    slice_rows = rows // n_steps
    assert slice_rows * n_steps == rows
    in_spec = pl.BlockSpec((None, slice_rows, cols), lambda *g: (layer, step_index(*g), 0))
    out_spec = pl.BlockSpec((slice_rows, cols), lambda *g: (step_index(*g), 0))
    return in_spec, out_spec, jax.ShapeDtypeStruct((rows, cols), BF16)


def _low_half(rows):
    return lax.broadcasted_iota(jnp.int32, (rows, LANES), 1) < HEAD_DIM


def _head_rmsnorm(x, gain_tiled):
    rows, width = x.shape
    low_half = _low_half(rows)
    cols = []
    for c in range(width // LANES):
        blk = x[:, c * LANES:(c + 1) * LANES]
        sq = blk * blk
        ss_lo = jnp.sum(jnp.where(low_half, sq, 0.0), axis=-1, keepdims=True)
        ss_hi = jnp.sum(jnp.where(low_half, 0.0, sq), axis=-1, keepdims=True)
        r_lo = lax.rsqrt(ss_lo * (1.0 / HEAD_DIM) + RMS_EPS)
        r_hi = lax.rsqrt(ss_hi * (1.0 / HEAD_DIM) + RMS_EPS)
        cols.append(blk * jnp.where(low_half, r_lo, r_hi))
    return jnp.concatenate(cols, axis=-1) * gain_tiled


def _split_heads(x):
    rows, width = x.shape
    low_half = _low_half(rows)
    lo, hi = [], []
    for c in range(width // LANES):
        col = x[:, c * LANES:(c + 1) * LANES]
        swapped = pltpu.roll(col, HEAD_DIM, axis=1)
        lo += [jnp.where(low_half, col, 0.0), jnp.where(low_half, swapped, 0.0)]
        hi += [jnp.where(low_half, 0.0, swapped), jnp.where(low_half, 0.0, col)]
    return jnp.concatenate(lo, axis=-1), jnp.concatenate(hi, axis=-1)


def _gelu_exact(x):
    return 0.5 * x * (1.0 + lax.erf(x * np.float32(1.0 / np.sqrt(2.0))))


def _silu(x):
    half = 0.5 * x
    return half + half * jnp.tanh(half)


def _inproj_kernel(x_ref, g_ref, w_ref, qn_ref, kn_ref, wout_slice_ref,
                   act_ref, wout_bf16_ref, *, d_attn, d_kv, d_gmlp):
    lay = _act_layout(d_attn, d_kv, d_gmlp)
    wout_bf16_ref[...] = wout_slice_ref[...].astype(BF16)
    x = x_ref[...]
    xg = (x * g_ref[...]).astype(BF16)
    r = lax.rsqrt(jnp.mean(x * x, axis=-1, keepdims=True) + RMS_EPS)

    def proj(lo, width):
        return r * jnp.dot(xg, w_ref[:, lo:lo + width], preferred_element_type=F32)

    def put(off, value):
        act_ref[:, off:off + value.shape[1]] = value.astype(BF16)

    q_off, k_off, v_off = 0, d_attn, d_attn + d_kv
    ga_off = v_off + d_kv
    u_off, vs_off, gb_off = ga_off + d_attn, ga_off + d_attn + d_gmlp, ga_off + d_attn + 2 * d_gmlp

    q = _head_rmsnorm(proj(q_off, d_attn), qn_ref[...])
    put(lay.q, q * (HEAD_DIM ** -0.5 * LOG2_E))
    k = _head_rmsnorm(proj(k_off, d_kv), kn_ref[...])
    for jb in range(k.shape[0] // WINDOW):
        for c in range(d_kv // LANES):
            tile = k[jb * WINDOW:(jb + 1) * WINDOW, c * LANES:(c + 1) * LANES]
            act_ref[jb * WINDOW:(jb + 1) * WINDOW,
                    lay.kt + c * LANES:lay.kt + (c + 1) * LANES] = tile.T.astype(BF16)
    vx, vy = _split_heads(proj(v_off, d_kv))
    put(lay.vx, vx)
    put(lay.vy, vy)
    put(lay.u, _gelu_exact(proj(u_off, d_gmlp)))
    put(lay.vs, _gelu_exact(proj(vs_off, d_gmlp)))
    put(lay.ga, _silu(proj(ga_off, d_attn)))
    put(lay.gb, _silu(proj(gb_off, d_gmlp)))


def _inproj(layer, x2d, norm_g, w_in_bf16, qn_tiled, kn_tiled, w_out, *, d_attn, d_kv, d_gmlp):
    t, d_model = x2d.shape
    d_in = w_in_bf16.shape[1]
    tm = TOKENS_PER_STEP
    width = _act_layout(d_attn, d_kv, d_gmlp).width
    row = lambda i: (i, 0)
    const = lambda i: (layer, 0, 0)
    cast_in, cast_out, cast_shape = _cast_slice_specs(w_out, layer, t // tm, lambda i: i)
    return pl.pallas_call(
        functools.partial(_inproj_kernel, d_attn=d_attn, d_kv=d_kv, d_gmlp=d_gmlp),
        grid=(t // tm,),
        in_specs=[
            pl.BlockSpec((tm, d_model), row),
            _resident((None, 1, d_model), const),
            _resident((d_model, d_in), lambda i: (0, 0)),
            _resident((None, 1, d_attn), const),
            _resident((None, 1, d_kv), const),
            cast_in,
        ],
        out_specs=[pl.BlockSpec((tm, width), row), cast_out],
        out_shape=[jax.ShapeDtypeStruct((t, width), BF16), cast_shape],
        compiler_params=_compiler_params(1),
        name="inproj",
    )(x2d, norm_g, w_in_bf16, qn_tiled, kn_tiled, w_out)


def _mixout_kernel(sinks_ref, x_ref, act_ref, prev_ref, ws_ref, bfull_ref, wout_ref, *rest,
                   layer, n_q_heads, n_gmlp_heads, tb, d_attn, convert_next_w_in):
    if convert_next_w_in:
        win_slice_ref, o_ref, win_bf16_ref, bias, wcat, ones, mix = rest
        win_bf16_ref[...] = win_slice_ref[...].astype(BF16)
    else:
        o_ref, bias, wcat, ones, mix = rest
    n_blocks = tb // WINDOW
    d_kv = N_KV_HEADS * HEAD_DIM
    d_gmlp = n_gmlp_heads * HEAD_DIM
    lay = _act_layout(d_attn, d_kv, d_gmlp)
    first_tile = (pl.program_id(0) == 0) & (pl.program_id(1) == 0)

    @pl.when(first_tile)
    def _init_tables():
        qpos = lax.broadcasted_iota(jnp.int32, (WINDOW, 2 * WINDOW), 0) + WINDOW
        kpos = lax.broadcasted_iota(jnp.int32, (WINDOW, 2 * WINDOW), 1)
        dist = qpos - kpos
        in_window = (dist >= 0) & (dist < WINDOW)
        dist_f = dist.astype(F32)
        for j in range(n_q_heads):
            slope = np.float32(2.0 ** (-8.0 * (j + 1) / n_q_heads))
            alibi = (-slope * LOG2_E) * dist_f
            rows = slice((j // 2 % 2) * WINDOW, (j // 2 % 2 + 1) * WINDOW)
            lanes = slice((j % 2) * 2 * WINDOW, (j % 2 + 1) * 2 * WINDOW)
            bias[1, j // 4, rows, lanes] = jnp.where(in_window, alibi, -jnp.inf)
            bias[0, j // 4, rows, lanes] = jnp.where(in_window & (kpos >= WINDOW), alibi, -jnp.inf)
        row = lax.broadcasted_iota(jnp.int32, (CHUNK, CHUNK), 0)
        col = lax.broadcasted_iota(jnp.int32, (CHUNK, CHUNK), 1)
        for hh in range(n_gmlp_heads):
            wcat[hh // 2, :, (hh % 2) * CHUNK:(hh % 2 + 1) * CHUNK] = (
                jnp.where(row >= col, ws_ref[hh], 0.0).astype(BF16))
        r = lax.broadcasted_iota(jnp.int32, (4 * WINDOW, LANES), 0)
        c = lax.broadcasted_iota(jnp.int32, (4 * WINDOW, LANES), 1)
        ones[...] = jnp.where((r < 2 * WINDOW) == (c < HEAD_DIM), 1.0, 0.0).astype(BF16)

    low_half = _low_half(WINDOW)
    has_prev0 = jnp.where(pl.program_id(1) == 0, 0, 1)
    zeros_kt = jnp.zeros((HEAD_DIM, 2 * WINDOW), BF16)

    def band(jb, rows_in_block, lanes):
        prev = (prev_ref[rows_in_block, lanes] if jb == 0 else
                act_ref[(jb - 1) * WINDOW + rows_in_block.start:
                        (jb - 1) * WINDOW + rows_in_block.stop, lanes])
        own = act_ref[jb * WINDOW + rows_in_block.start:jb * WINDOW + rows_in_block.stop, lanes]
        return [prev, own]

    all_rows = slice(0, WINDOW)

    def block_rows(jb):
        return slice(jb * WINDOW, (jb + 1) * WINDOW)

    def q_cols(hk):
        return [slice((2 * hk + pr) * LANES, (2 * hk + pr + 1) * LANES) for pr in range(2)]

    def scores(jb, hk):
        feat_rows = slice((hk % 2) * HEAD_DIM, (hk % 2 + 1) * HEAD_DIM)
        kt_lanes = slice(lay.kt + (hk // 2) * LANES, lay.kt + (hk // 2 + 1) * LANES)
        kband = jnp.concatenate(band(jb, feat_rows, kt_lanes), axis=1)
        krhs = jnp.concatenate([jnp.concatenate([kband, zeros_kt], axis=1),
                                jnp.concatenate([zeros_kt, kband], axis=1)], axis=0)
        q2 = jnp.concatenate([act_ref[block_rows(jb), lay.q + cl.start:lay.q + cl.stop]
                              for cl in q_cols(hk)], axis=0)
        s = jnp.dot(q2, krhs, preferred_element_type=F32)
        return s + (bias[has_prev0, hk] if jb == 0 else bias[1, hk])

    def softmax_numerators(hk, s):
        probs, sink_terms = [], []
        for pr in range(2):
            sink_a = sinks_ref[layer, 4 * hk + 2 * pr] * LOG2_E
            sink_b = sinks_ref[layer, 4 * hk + 2 * pr + 1] * LOG2_E
            s_a = s[pr * WINDOW:(pr + 1) * WINDOW, :2 * WINDOW]
            s_b = s[pr * WINDOW:(pr + 1) * WINDOW, 2 * WINDOW:]
            m_a = jnp.maximum(jnp.max(s_a, axis=-1, keepdims=True), sink_a)
            m_b = jnp.maximum(jnp.max(s_b, axis=-1, keepdims=True), sink_b)
            probs.append(jnp.concatenate([jnp.exp2(s_a - m_a), jnp.exp2(s_b - m_b)],
                                         axis=1).astype(BF16))
            sink_terms.append(jnp.where(low_half, jnp.exp2(sink_a - m_a),
                                        jnp.exp2(sink_b - m_b)))
        return jnp.concatenate(probs, axis=0), sink_terms

    def weighted_values(jb, hk, probs):
        vx_lanes = slice(lay.vx + hk * LANES, lay.vx + (hk + 1) * LANES)
        vy_lanes = slice(lay.vy + hk * LANES, lay.vy + (hk + 1) * LANES)
        vstack = jnp.concatenate(band(jb, all_rows, vx_lanes) + band(jb, all_rows, vy_lanes),
                                 axis=0)
        vrhs = jnp.concatenate([vstack, ones[...]], axis=1)
        return jnp.dot(probs, vrhs, preferred_element_type=F32)

    def normalize_and_gate(jb, hk, o, sink_terms):
        rows = block_rows(jb)
        for pr, cl in enumerate(q_cols(hk)):
            o_pr = o[pr * WINDOW:(pr + 1) * WINDOW, :]
            attn = o_pr[:, :LANES] / (o_pr[:, LANES:] + sink_terms[pr])
            ga = act_ref[rows, lay.ga + cl.start:lay.ga + cl.stop]
            mix[rows, cl] = attn.astype(BF16) * ga

    def spatial_gating_pair(jb, jp):
        rows = block_rows(jb)
        cl = slice(jp * LANES, (jp + 1) * LANES)
        vpair = act_ref[rows, lay.vs + cl.start:lay.vs + cl.stop]
        zero = jnp.zeros_like(vpair)
        rhs = jnp.concatenate([jnp.where(low_half, vpair, zero),
                               jnp.where(low_half, zero, vpair)], axis=0)
        mixed = jnp.dot(wcat[jp], rhs, preferred_element_type=F32) + bfull_ref[:, cl]
        u = act_ref[rows, lay.u + cl.start:lay.u + cl.stop]
        gb = act_ref[rows, lay.gb + cl.start:lay.gb + cl.stop]
        mix[rows, d_attn + cl.start:d_attn + cl.stop] = mixed.astype(BF16) * (u * gb)

    def out_rows(jo):
        rows = slice(jo * OUT_ROWS, (jo + 1) * OUT_ROWS)
        o_ref[rows, :] = x_ref[rows, :] + jnp.dot(mix[rows, :], wout_ref[...],
                                                  preferred_element_type=F32)

    blocks_per_out = OUT_ROWS // WINDOW
    pairs_per_head = n_gmlp_heads // 2 // N_KV_HEADS
    for jo in range(n_blocks // blocks_per_out):
        items = [(jb, hk) for jb in range(jo * blocks_per_out, (jo + 1) * blocks_per_out)
                 for hk in range(N_KV_HEADS)]
        s_queue = [scores(*item) for item in items[:SCORES_AHEAD]]
        pending = None
        for idx, (jb, hk) in enumerate(items):
            if idx + SCORES_AHEAD < len(items):
                s_queue.append(scores(*items[idx + SCORES_AHEAD]))
            probs, sink_terms = softmax_numerators(hk, s_queue.pop(0))
            o = weighted_values(jb, hk, probs)
            if pending is not None:
                normalize_and_gate(*pending)
            pending = (jb, hk, o, sink_terms)
            for jp in range(hk * pairs_per_head, (hk + 1) * pairs_per_head):
                spatial_gating_pair(jb, jp)
        normalize_and_gate(*pending)
        out_rows(jo)


def _mixout(layer, sinks, x2d, act, w_s, b_full, w_out_bf16, w_in, *, batch, seq, d_attn, d_gmlp):
    t, d_model = x2d.shape
    convert_next_w_in = layer + 1 < w_in.shape[0]
    d_kv = N_KV_HEADS * HEAD_DIM
    lay = _act_layout(d_attn, d_kv, d_gmlp)
    n_q_heads = d_attn // HEAD_DIM
    n_gmlp_heads = d_gmlp // HEAD_DIM
    tb = TOKENS_PER_STEP
    tiles = seq // tb
    blocks_per_tile = tb // WINDOW
    row = lambda b, i: (b * tiles + i, 0)
    prev = lambda b, i: (b * (seq // WINDOW) + jnp.maximum(i * blocks_per_tile - 1, 0), 0)
    in_specs = [
        pl.BlockSpec(memory_space=pltpu.SMEM),
        pl.BlockSpec((tb, d_model), row),
        pl.BlockSpec((tb, lay.width), row),
        pl.BlockSpec((WINDOW, lay.band_width), prev),
        _resident((None, n_gmlp_heads, CHUNK, CHUNK), lambda b, i: (layer, 0, 0, 0)),
        _resident((None, CHUNK, d_gmlp), lambda b, i: (layer, 0, 0)),
        _resident((d_attn + d_gmlp, d_model), lambda b, i: (0, 0)),
    ]
    operands = [sinks, x2d, act, act, w_s, b_full, w_out_bf16]
    out_specs = [pl.BlockSpec((tb, d_model), row)]
    out_shape = [jax.ShapeDtypeStruct((t, d_model), F32)]
    if convert_next_w_in:
        cast_in, cast_out, cast_shape = _cast_slice_specs(
            w_in, layer + 1, batch * tiles, lambda b, i: b * tiles + i)
        in_specs.append(cast_in)
        operands.append(w_in)
        out_specs.append(cast_out)
        out_shape.append(cast_shape)
    outs = pl.pallas_call(
        functools.partial(_mixout_kernel, layer=layer, n_q_heads=n_q_heads,
                          n_gmlp_heads=n_gmlp_heads, tb=tb, d_attn=d_attn,
                          convert_next_w_in=convert_next_w_in),
        grid=(batch, tiles),
        in_specs=in_specs,
        out_specs=out_specs,
        out_shape=out_shape,
        scratch_shapes=[
            pltpu.VMEM((2, N_KV_HEADS, 2 * WINDOW, 4 * WINDOW), F32),
            pltpu.VMEM((n_gmlp_heads // 2, CHUNK, 2 * CHUNK), BF16),
            pltpu.VMEM((4 * WINDOW, LANES), BF16),
            pltpu.VMEM((tb, d_attn + d_gmlp), BF16),
        ],
        compiler_params=_compiler_params(2),
        name="mixout",
    )(*operands)
    return outs if convert_next_w_in else (outs[0], None)


def kernel(x, norm_g, w_in, q_norm, k_norm, sinks, w_s, b_s, w_out):
    batch, seq, d_model = x.shape
    depth = norm_g.shape[0]
    n_q_heads = sinks.shape[1]
    n_gmlp_heads = w_s.shape[1]
    d_attn = n_q_heads * HEAD_DIM
    d_gmlp = n_gmlp_heads * HEAD_DIM
    d_kv = N_KV_HEADS * HEAD_DIM
    assert w_in.shape[2] == 2 * d_attn + 2 * d_kv + 3 * d_gmlp
    assert seq % TOKENS_PER_STEP == 0 and q_norm.shape[1] == HEAD_DIM
    assert n_q_heads == 4 * N_KV_HEADS and TOKENS_PER_STEP % OUT_ROWS == 0

    norm_g3 = norm_g.reshape(depth, 1, d_model)
    qn_tiled = jnp.tile(q_norm, (1, n_q_heads)).reshape(depth, 1, d_attn)
    kn_tiled = jnp.tile(k_norm, (1, N_KV_HEADS)).reshape(depth, 1, d_kv)
    b_full = jnp.repeat(jnp.swapaxes(b_s, 1, 2), HEAD_DIM, axis=2)

    x2d = x.reshape(batch * seq, d_model)
    w_in_bf16 = w_in[0].astype(BF16)
    for l in range(depth):
        act, w_out_bf16 = _inproj(l, x2d, norm_g3, w_in_bf16, qn_tiled, kn_tiled, w_out,
                                  d_attn=d_attn, d_kv=d_kv, d_gmlp=d_gmlp)
        x2d, w_in_bf16 = _mixout(l, sinks, x2d, act, w_s, b_full, w_out_bf16, w_in,
                                 batch=batch, seq=seq, d_attn=d_attn, d_gmlp=d_gmlp)
    return x2d.reshape(batch, seq, d_model)
```

```python
import collections
import functools

import jax
import jax.numpy as jnp
import numpy as np
from jax import lax
from jax.experimental import pallas as pl
from jax.experimental.pallas import tpu as pltpu

HEAD_DIM = 64
N_KV_HEADS = 4
WINDOW = 128
CHUNK = 128
RMS_EPS = 1e-6
LANES = 128
VMEM_LIMIT_BYTES = 56 * 1024 * 1024
TOKENS_PER_STEP = 512
OUT_ROWS = 512
SCORES_AHEAD = 2
LOG2_E = float(np.log2(np.e))

F32 = jnp.float32
BF16 = jnp.bfloat16

ActLayout = collections.namedtuple("ActLayout", "kt vx vy q ga u vs gb width band_width")


def _act_layout(d_attn, d_kv, d_gmlp):
    kt, vx = 0, d_kv
    vy = vx + 2 * d_kv
    q = vy + 2 * d_kv
    ga = q + d_attn
    u = ga + d_attn
    vs = u + d_gmlp
    gb = vs + d_gmlp
    return ActLayout(kt, vx, vy, q, ga, u, vs, gb, width=gb + d_gmlp, band_width=q)


def _compiler_params(n_grid_dims):
    return pltpu.CompilerParams(
        dimension_semantics=("arbitrary",) * n_grid_dims,
        vmem_limit_bytes=VMEM_LIMIT_BYTES,
    )


def _resident(block_shape, index_map):
    return pl.BlockSpec(block_shape, index_map, pipeline_mode=pl.Buffered(1))


def _cast_slice_specs(weight, layer, n_steps, step_index):
    _, rows, cols = weight.shape
    slice_rows = rows // n_steps
    assert slice_rows * n_steps == rows
    in_spec = pl.BlockSpec((None, slice_rows, cols), lambda *g: (layer, step_index(*g), 0))
    out_spec = pl.BlockSpec((slice_rows, cols), lambda *g: (step_index(*g), 0))
    return in_spec, out_spec, jax.ShapeDtypeStruct((rows, cols), BF16)


def _low_half(rows):
    return lax.broadcasted_iota(jnp.int32, (rows, LANES), 1) < HEAD_DIM


def _head_rmsnorm(x, gain_tiled):
    rows, width = x.shape
    low_half = _low_half(rows)
    cols = []
    for c in range(width // LANES):
        blk = x[:, c * LANES:(c + 1) * LANES]
        sq = blk * blk
        ss_lo = jnp.sum(jnp.where(low_half, sq, 0.0), axis=-1, keepdims=True)
        ss_hi = jnp.sum(jnp.where(low_half, 0.0, sq), axis=-1, keepdims=True)
        r_lo = lax.rsqrt(ss_lo * (1.0 / HEAD_DIM) + RMS_EPS)
        r_hi = lax.rsqrt(ss_hi * (1.0 / HEAD_DIM) + RMS_EPS)
        cols.append(blk * jnp.where(low_half, r_lo, r_hi))
    return jnp.concatenate(cols, axis=-1) * gain_tiled


def _split_heads(x):
    rows, width = x.shape
    low_half = _low_half(rows)
    lo, hi = [], []
    for c in range(width // LANES):
        col = x[:, c * LANES:(c + 1) * LANES]
        swapped = pltpu.roll(col, HEAD_DIM, axis=1)
        lo += [jnp.where(low_half, col, 0.0), jnp.where(low_half, swapped, 0.0)]
        hi += [jnp.where(low_half, 0.0, swapped), jnp.where(low_half, 0.0, col)]
    return jnp.concatenate(lo, axis=-1), jnp.concatenate(hi, axis=-1)


def _gelu_exact(x):
    return 0.5 * x * (1.0 + lax.erf(x * np.float32(1.0 / np.sqrt(2.0))))


def _silu(x):
    half = 0.5 * x
    return half + half * jnp.tanh(half)


def _inproj_kernel(x_ref, g_ref, w_ref, qn_ref, kn_ref, wout_slice_ref,
                   act_ref, wout_bf16_ref, *, d_attn, d_kv, d_gmlp):
    lay = _act_layout(d_attn, d_kv, d_gmlp)
    wout_bf16_ref[...] = wout_slice_ref[...].astype(BF16)
    x = x_ref[...]
    xg = (x * g_ref[...]).astype(BF16)
    r = lax.rsqrt(jnp.mean(x * x, axis=-1, keepdims=True) + RMS_EPS)

    def proj(lo, width):
        return r * jnp.dot(xg, w_ref[:, lo:lo + width], preferred_element_type=F32)

    def put(off, value):
        act_ref[:, off:off + value.shape[1]] = value.astype(BF16)

    q_off, k_off, v_off = 0, d_attn, d_attn + d_kv
    ga_off = v_off + d_kv
    u_off, vs_off, gb_off = ga_off + d_attn, ga_off + d_attn + d_gmlp, ga_off + d_attn + 2 * d_gmlp

    q = _head_rmsnorm(proj(q_off, d_attn), qn_ref[...])
    put(lay.q, q * (HEAD_DIM ** -0.5 * LOG2_E))
    k = _head_rmsnorm(proj(k_off, d_kv), kn_ref[...])
    for jb in range(k.shape[0] // WINDOW):
        for c in range(d_kv // LANES):
            tile = k[jb * WINDOW:(jb + 1) * WINDOW, c * LANES:(c + 1) * LANES]
            act_ref[jb * WINDOW:(jb + 1) * WINDOW,
                    lay.kt + c * LANES:lay.kt + (c + 1) * LANES] = tile.T.astype(BF16)
    vx, vy = _split_heads(proj(v_off, d_kv))
    put(lay.vx, vx)
    put(lay.vy, vy)
    put(lay.u, _gelu_exact(proj(u_off, d_gmlp)))
    put(lay.vs, _gelu_exact(proj(vs_off, d_gmlp)))
    put(lay.ga, _silu(proj(ga_off, d_attn)))
    put(lay.gb, _silu(proj(gb_off, d_gmlp)))


def _inproj(layer, x2d, norm_g, w_in_bf16, qn_tiled, kn_tiled, w_out, *, d_attn, d_kv, d_gmlp):
    t, d_model = x2d.shape
    d_in = w_in_bf16.shape[1]
    tm = TOKENS_PER_STEP
    width = _act_layout(d_attn, d_kv, d_gmlp).width
    row = lambda i: (i, 0)
    const = lambda i: (layer, 0, 0)
    cast_in, cast_out, cast_shape = _cast_slice_specs(w_out, layer, t // tm, lambda i: i)
    return pl.pallas_call(
        functools.partial(_inproj_kernel, d_attn=d_attn, d_kv=d_kv, d_gmlp=d_gmlp),
        grid=(t // tm,),
        in_specs=[
            pl.BlockSpec((tm, d_model), row),
            _resident((None, 1, d_model), const),
            _resident((d_model, d_in), lambda i: (0, 0)),
            _resident((None, 1, d_attn), const),
            _resident((None, 1, d_kv), const),
            cast_in,
        ],
        out_specs=[pl.BlockSpec((tm, width), row), cast_out],
        out_shape=[jax.ShapeDtypeStruct((t, width), BF16), cast_shape],
        compiler_params=_compiler_params(1),
        name="inproj",
    )(x2d, norm_g, w_in_bf16, qn_tiled, kn_tiled, w_out)


def _mixout_kernel(sinks_ref, x_ref, act_ref, prev_ref, ws_ref, bfull_ref, wout_ref, *rest,
                   layer, n_q_heads, n_gmlp_heads, tb, d_attn, convert_next_w_in):
    if convert_next_w_in:
        win_slice_ref, o_ref, win_bf16_ref, bias, wcat, ones, mix = rest
        win_bf16_ref[...] = win_slice_ref[...].astype(BF16)
    else:
        o_ref, bias, wcat, ones, mix = rest
    n_blocks = tb // WINDOW
    d_kv = N_KV_HEADS * HEAD_DIM
    d_gmlp = n_gmlp_heads * HEAD_DIM
    lay = _act_layout(d_attn, d_kv, d_gmlp)
    first_tile = (pl.program_id(0) == 0) & (pl.program_id(1) == 0)

    @pl.when(first_tile)
    def _init_tables():
        qpos = lax.broadcasted_iota(jnp.int32, (WINDOW, 2 * WINDOW), 0) + WINDOW
        kpos = lax.broadcasted_iota(jnp.int32, (WINDOW, 2 * WINDOW), 1)
        dist = qpos - kpos
        in_window = (dist >= 0) & (dist < WINDOW)
        dist_f = dist.astype(F32)
        for j in range(n_q_heads):
            slope = np.float32(2.0 ** (-8.0 * (j + 1) / n_q_heads))
            alibi = (-slope * LOG2_E) * dist_f
            rows = slice((j // 2 % 2) * WINDOW, (j // 2 % 2 + 1) * WINDOW)
            lanes = slice((j % 2) * 2 * WINDOW, (j % 2 + 1) * 2 * WINDOW)
            bias[1, j // 4, rows, lanes] = jnp.where(in_window, alibi, -jnp.inf)
            bias[0, j // 4, rows, lanes] = jnp.where(in_window & (kpos >= WINDOW), alibi, -jnp.inf)
        row = lax.broadcasted_iota(jnp.int32, (CHUNK, CHUNK), 0)
        col = lax.broadcasted_iota(jnp.int32, (CHUNK, CHUNK), 1)
        for hh in range(n_gmlp_heads):
            wcat[hh // 2, :, (hh % 2) * CHUNK:(hh % 2 + 1) * CHUNK] = (
                jnp.where(row >= col, ws_ref[hh], 0.0).astype(BF16))
        r = lax.broadcasted_iota(jnp.int32, (4 * WINDOW, LANES), 0)
        c = lax.broadcasted_iota(jnp.int32, (4 * WINDOW, LANES), 1)
        ones[...] = jnp.where((r < 2 * WINDOW) == (c < HEAD_DIM), 1.0, 0.0).astype(BF16)

    low_half = _low_half(WINDOW)
    has_prev0 = jnp.where(pl.program_id(1) == 0, 0, 1)
    zeros_kt = jnp.zeros((HEAD_DIM, 2 * WINDOW), BF16)

    def band(jb, rows_in_block, lanes):
        prev = (prev_ref[rows_in_block, lanes] if jb == 0 else
                act_ref[(jb - 1) * WINDOW + rows_in_block.start:
                        (jb - 1) * WINDOW + rows_in_block.stop, lanes])
        own = act_ref[jb * WINDOW + rows_in_block.start:jb * WINDOW + rows_in_block.stop, lanes]
        return [prev, own]

    all_rows = slice(0, WINDOW)

    def block_rows(jb):
        return slice(jb * WINDOW, (jb + 1) * WINDOW)

    def q_cols(hk):
        return [slice((2 * hk + pr) * LANES, (2 * hk + pr + 1) * LANES) for pr in range(2)]

    def scores(jb, hk):
        feat_rows = slice((hk % 2) * HEAD_DIM, (hk % 2 + 1) * HEAD_DIM)
        kt_lanes = slice(lay.kt + (hk // 2) * LANES, lay.kt + (hk // 2 + 1) * LANES)
        kband = jnp.concatenate(band(jb, feat_rows, kt_lanes), axis=1)
        krhs = jnp.concatenate([jnp.concatenate([kband, zeros_kt], axis=1),
                                jnp.concatenate([zeros_kt, kband], axis=1)], axis=0)
        q2 = jnp.concatenate([act_ref[block_rows(jb), lay.q + cl.start:lay.q + cl.stop]
                              for cl in q_cols(hk)], axis=0)
        s = jnp.dot(q2, krhs, preferred_element_type=F32)
        return s + (bias[has_prev0, hk] if jb == 0 else bias[1, hk])

    def softmax_numerators(hk, s):
        probs, sink_terms = [], []
        for pr in range(2):
            sink_a = sinks_ref[layer, 4 * hk + 2 * pr] * LOG2_E
            sink_b = sinks_ref[layer, 4 * hk + 2 * pr + 1] * LOG2_E
            s_a = s[pr * WINDOW:(pr + 1) * WINDOW, :2 * WINDOW]
            s_b = s[pr * WINDOW:(pr + 1) * WINDOW, 2 * WINDOW:]
            m_a = jnp.maximum(jnp.max(s_a, axis=-1, keepdims=True), sink_a)
            m_b = jnp.maximum(jnp.max(s_b, axis=-1, keepdims=True), sink_b)
            probs.append(jnp.concatenate([jnp.exp2(s_a - m_a), jnp.exp2(s_b - m_b)],
                                         axis=1).astype(BF16))
            sink_terms.append(jnp.where(low_half, jnp.exp2(sink_a - m_a),
                                        jnp.exp2(sink_b - m_b)))
        return jnp.concatenate(probs, axis=0), sink_terms

    def weighted_values(jb, hk, probs):
        vx_lanes = slice(lay.vx + hk * LANES, lay.vx + (hk + 1) * LANES)
        vy_lanes = slice(lay.vy + hk * LANES, lay.vy + (hk + 1) * LANES)
        vstack = jnp.concatenate(band(jb, all_rows, vx_lanes) + band(jb, all_rows, vy_lanes),
                                 axis=0)
        vrhs = jnp.concatenate([vstack, ones[...]], axis=1)
        return jnp.dot(probs, vrhs, preferred_element_type=F32)

    def normalize_and_gate(jb, hk, o, sink_terms):
        rows = block_rows(jb)
        for pr, cl in enumerate(q_cols(hk)):
            o_pr = o[pr * WINDOW:(pr + 1) * WINDOW, :]
            attn = o_pr[:, :LANES] / (o_pr[:, LANES:] + sink_terms[pr])
            ga = act_ref[rows, lay.ga + cl.start:lay.ga + cl.stop]
            mix[rows, cl] = attn.astype(BF16) * ga

    def spatial_gating_pair(blocks, jp):
        cl = slice(jp * LANES, (jp + 1) * LANES)
        tiles = []
        for jb in blocks:
            vpair = act_ref[block_rows(jb), lay.vs + cl.start:lay.vs + cl.stop]
            zero = jnp.zeros_like(vpair)
            tiles.append(jnp.concatenate([jnp.where(low_half, vpair, zero),
                                          jnp.where(low_half, zero, vpair)], axis=0))
        mixed = jnp.dot(wcat[jp], jnp.concatenate(tiles, axis=1),
                        preferred_element_type=F32)
        for n, jb in enumerate(blocks):
            rows = block_rows(jb)
            mixed_jb = mixed[:, n * LANES:(n + 1) * LANES] + bfull_ref[:, cl]
            u = act_ref[rows, lay.u + cl.start:lay.u + cl.stop]
            gb = act_ref[rows, lay.gb + cl.start:lay.gb + cl.stop]
            mix[rows, d_attn + cl.start:d_attn + cl.stop] = mixed_jb.astype(BF16) * (u * gb)

    def out_rows(jo):
        rows = slice(jo * OUT_ROWS, (jo + 1) * OUT_ROWS)
        o_ref[rows, :] = x_ref[rows, :] + jnp.dot(mix[rows, :], wout_ref[...],
                                                  preferred_element_type=F32)

    blocks_per_out = OUT_ROWS // WINDOW
    for jo in range(n_blocks // blocks_per_out):
        blocks = range(jo * blocks_per_out, (jo + 1) * blocks_per_out)
        items = [(jb, hk) for jb in blocks for hk in range(N_KV_HEADS)]
        gating_pairs = list(range(n_gmlp_heads // 2))
        s_queue = [scores(*item) for item in items[:SCORES_AHEAD]]
        pending = None
        for idx, (jb, hk) in enumerate(items):
            if idx + SCORES_AHEAD < len(items):
                s_queue.append(scores(*items[idx + SCORES_AHEAD]))
            probs, sink_terms = softmax_numerators(hk, s_queue.pop(0))
            o = weighted_values(jb, hk, probs)
            if pending is not None:
                normalize_and_gate(*pending)
            pending = (jb, hk, o, sink_terms)
            if idx >= len(items) - len(gating_pairs) and gating_pairs:
                spatial_gating_pair(blocks, gating_pairs.pop(0))
        normalize_and_gate(*pending)
        for jp in gating_pairs:
            spatial_gating_pair(blocks, jp)
        out_rows(jo)


def _mixout(layer, sinks, x2d, act, w_s, b_full, w_out_bf16, w_in, *, batch, seq, d_attn, d_gmlp):
    t, d_model = x2d.shape
    convert_next_w_in = layer + 1 < w_in.shape[0]
    d_kv = N_KV_HEADS * HEAD_DIM
    lay = _act_layout(d_attn, d_kv, d_gmlp)
    n_q_heads = d_attn // HEAD_DIM
    n_gmlp_heads = d_gmlp // HEAD_DIM
    tb = TOKENS_PER_STEP
    tiles = seq // tb
    blocks_per_tile = tb // WINDOW
    row = lambda b, i: (b * tiles + i, 0)
    prev = lambda b, i: (b * (seq // WINDOW) + jnp.maximum(i * blocks_per_tile - 1, 0), 0)
    in_specs = [
        pl.BlockSpec(memory_space=pltpu.SMEM),
        pl.BlockSpec((tb, d_model), row),
        pl.BlockSpec((tb, lay.width), row),
        pl.BlockSpec((WINDOW, lay.band_width), prev),
        _resident((None, n_gmlp_heads, CHUNK, CHUNK), lambda b, i: (layer, 0, 0, 0)),
        _resident((None, CHUNK, d_gmlp), lambda b, i: (layer, 0, 0)),
        _resident((d_attn + d_gmlp, d_model), lambda b, i: (0, 0)),
    ]
    operands = [sinks, x2d, act, act, w_s, b_full, w_out_bf16]
    out_specs = [pl.BlockSpec((tb, d_model), row)]
    out_shape = [jax.ShapeDtypeStruct((t, d_model), F32)]
    if convert_next_w_in:
        cast_in, cast_out, cast_shape = _cast_slice_specs(
            w_in, layer + 1, batch * tiles, lambda b, i: b * tiles + i)
        in_specs.append(cast_in)
        operands.append(w_in)
        out_specs.append(cast_out)
        out_shape.append(cast_shape)
    outs = pl.pallas_call(
        functools.partial(_mixout_kernel, layer=layer, n_q_heads=n_q_heads,
                          n_gmlp_heads=n_gmlp_heads, tb=tb, d_attn=d_attn,
                          convert_next_w_in=convert_next_w_in),
        grid=(batch, tiles),
        in_specs=in_specs,
        out_specs=out_specs,
        out_shape=out_shape,
        scratch_shapes=[
            pltpu.VMEM((2, N_KV_HEADS, 2 * WINDOW, 4 * WINDOW), F32),
            pltpu.VMEM((n_gmlp_heads // 2, CHUNK, 2 * CHUNK), BF16),
            pltpu.VMEM((4 * WINDOW, LANES), BF16),
            pltpu.VMEM((tb, d_attn + d_gmlp), BF16),
        ],
        compiler_params=_compiler_params(2),
        name="mixout",
    )(*operands)
    return outs if convert_next_w_in else (outs[0], None)


def kernel(x, norm_g, w_in, q_norm, k_norm, sinks, w_s, b_s, w_out):
    batch, seq, d_model = x.shape
    depth = norm_g.shape[0]
    n_q_heads = sinks.shape[1]
    n_gmlp_heads = w_s.shape[1]
    d_attn = n_q_heads * HEAD_DIM
    d_gmlp = n_gmlp_heads * HEAD_DIM
    d_kv = N_KV_HEADS * HEAD_DIM
    assert w_in.shape[2] == 2 * d_attn + 2 * d_kv + 3 * d_gmlp
    assert seq % TOKENS_PER_STEP == 0 and q_norm.shape[1] == HEAD_DIM
    assert n_q_heads == 4 * N_KV_HEADS and TOKENS_PER_STEP % OUT_ROWS == 0

    norm_g3 = norm_g.reshape(depth, 1, d_model)
    qn_tiled = jnp.tile(q_norm, (1, n_q_heads)).reshape(depth, 1, d_attn)
    kn_tiled = jnp.tile(k_norm, (1, N_KV_HEADS)).reshape(depth, 1, d_kv)
    b_full = jnp.repeat(jnp.swapaxes(b_s, 1, 2), HEAD_DIM, axis=2)

    x2d = x.reshape(batch * seq, d_model)
    w_in_bf16 = w_in[0].astype(BF16)
    for l in range(depth):
        act, w_out_bf16 = _inproj(l, x2d, norm_g3, w_in_bf16, qn_tiled, kn_tiled, w_out,
                                  d_attn=d_attn, d_kv=d_kv, d_gmlp=d_gmlp)
        x2d, w_in_bf16 = _mixout(l, sinks, x2d, act, w_s, b_full, w_out_bf16, w_in,
                                 batch=batch, seq=seq, d_attn=d_attn, d_gmlp=d_gmlp)
    return x2d.reshape(batch, seq, d_model)
```

```python
import collections
import functools

import jax
import jax.numpy as jnp
import numpy as np
from jax import lax
from jax.experimental import pallas as pl
from jax.experimental.pallas import tpu as pltpu

HEAD_DIM = 64
N_KV_HEADS = 4
WINDOW = 128
CHUNK = 128
RMS_EPS = 1e-6
LANES = 128
VMEM_LIMIT_BYTES = 56 * 1024 * 1024
TOKENS_PER_STEP = 512
OUT_ROWS = 512
SCORES_AHEAD = 2
LOG2_E = float(np.log2(np.e))

F32 = jnp.float32
BF16 = jnp.bfloat16

ActLayout = collections.namedtuple("ActLayout", "kt vx vy q ga u vs gb width band_width")


def _act_layout(d_attn, d_kv, d_gmlp):
    kt, vx = 0, d_kv
    vy = vx + 2 * d_kv
    q = vy + 2 * d_kv
    ga = q + d_attn
    u = ga + d_attn
    vs = u + d_gmlp
    gb = vs + d_gmlp
    return ActLayout(kt, vx, vy, q, ga, u, vs, gb, width=gb + d_gmlp, band_width=q)


def _compiler_params(n_grid_dims):
    return pltpu.CompilerParams(
        dimension_semantics=("arbitrary",) * n_grid_dims,
        vmem_limit_bytes=VMEM_LIMIT_BYTES,
    )


def _resident(block_shape, index_map):
    return pl.BlockSpec(block_shape, index_map, pipeline_mode=pl.Buffered(1))


def _cast_slice_specs(weight, layer, n_steps, step_index):
    _, rows, cols = weight.shape
    slice_rows = rows // n_steps
    assert slice_rows * n_steps == rows
    in_spec = pl.BlockSpec((None, slice_rows, cols), lambda *g: (layer, step_index(*g), 0))
    out_spec = pl.BlockSpec((slice_rows, cols), lambda *g: (step_index(*g), 0))
    return in_spec, out_spec, jax.ShapeDtypeStruct((rows, cols), BF16)


def _low_half(rows):
    return lax.broadcasted_iota(jnp.int32, (rows, LANES), 1) < HEAD_DIM


def _head_rmsnorm(x, gain_tiled):
    rows, width = x.shape
    low_half = _low_half(rows)
    cols = []
    for c in range(width // LANES):
        blk = x[:, c * LANES:(c + 1) * LANES]
        sq = blk * blk
        ss_lo = jnp.sum(jnp.where(low_half, sq, 0.0), axis=-1, keepdims=True)
        ss_hi = jnp.sum(jnp.where(low_half, 0.0, sq), axis=-1, keepdims=True)
        r_lo = lax.rsqrt(ss_lo * (1.0 / HEAD_DIM) + RMS_EPS)
        r_hi = lax.rsqrt(ss_hi * (1.0 / HEAD_DIM) + RMS_EPS)
        cols.append(blk * jnp.where(low_half, r_lo, r_hi))
    return jnp.concatenate(cols, axis=-1) * gain_tiled


def _split_heads(x):
    rows, width = x.shape
    low_half = _low_half(rows)
    lo, hi = [], []
    for c in range(width // LANES):
        col = x[:, c * LANES:(c + 1) * LANES]
        swapped = pltpu.roll(col, HEAD_DIM, axis=1)
        lo += [jnp.where(low_half, col, 0.0), jnp.where(low_half, swapped, 0.0)]
        hi += [jnp.where(low_half, 0.0, swapped), jnp.where(low_half, 0.0, col)]
    return jnp.concatenate(lo, axis=-1), jnp.concatenate(hi, axis=-1)


def _gelu_exact(x):
    return 0.5 * x * (1.0 + lax.erf(x * np.float32(1.0 / np.sqrt(2.0))))


def _silu(x):
    half = 0.5 * x
    return half + half * jnp.tanh(half)


def _inproj_kernel(x_ref, g_ref, w_ref, qn_ref, kn_ref, wout_slice_ref,
                   act_ref, wout_bf16_ref, *, d_attn, d_kv, d_gmlp):
    lay = _act_layout(d_attn, d_kv, d_gmlp)
    wout_bf16_ref[...] = wout_slice_ref[...].astype(BF16)
    x = x_ref[...]
    xg = (x * g_ref[...]).astype(BF16)
    r = lax.rsqrt(jnp.mean(x * x, axis=-1, keepdims=True) + RMS_EPS)

    def proj(lo, width):
        return r * jnp.dot(xg, w_ref[:, lo:lo + width], preferred_element_type=F32)

    def put(off, value):
        act_ref[:, off:off + value.shape[1]] = value.astype(BF16)

    q_off, k_off, v_off = 0, d_attn, d_attn + d_kv
    ga_off = v_off + d_kv
    u_off, vs_off, gb_off = ga_off + d_attn, ga_off + d_attn + d_gmlp, ga_off + d_attn + 2 * d_gmlp

    q = _head_rmsnorm(proj(q_off, d_attn), qn_ref[...])
    put(lay.q, q * (HEAD_DIM ** -0.5 * LOG2_E))
    k = _head_rmsnorm(proj(k_off, d_kv), kn_ref[...])
    for jb in range(k.shape[0] // WINDOW):
        for c in range(d_kv // LANES):
            tile = k[jb * WINDOW:(jb + 1) * WINDOW, c * LANES:(c + 1) * LANES]
            act_ref[jb * WINDOW:(jb + 1) * WINDOW,
                    lay.kt + c * LANES:lay.kt + (c + 1) * LANES] = tile.T.astype(BF16)
    vx, vy = _split_heads(proj(v_off, d_kv))
    put(lay.vx, vx)
    put(lay.vy, vy)
    put(lay.u, _gelu_exact(proj(u_off, d_gmlp)))
    put(lay.vs, _gelu_exact(proj(vs_off, d_gmlp)))
    put(lay.ga, _silu(proj(ga_off, d_attn)))
    put(lay.gb, _silu(proj(gb_off, d_gmlp)))


def _inproj(layer, x2d, norm_g, w_in_bf16, qn_tiled, kn_tiled, w_out, *, d_attn, d_kv, d_gmlp):
    t, d_model = x2d.shape
    d_in = w_in_bf16.shape[1]
    tm = TOKENS_PER_STEP
    width = _act_layout(d_attn, d_kv, d_gmlp).width
    row = lambda i: (i, 0)
    const = lambda i: (layer, 0, 0)
    cast_in, cast_out, cast_shape = _cast_slice_specs(w_out, layer, t // tm, lambda i: i)
    return pl.pallas_call(
        functools.partial(_inproj_kernel, d_attn=d_attn, d_kv=d_kv, d_gmlp=d_gmlp),
        grid=(t // tm,),
        in_specs=[
            pl.BlockSpec((tm, d_model), row),
            _resident((None, 1, d_model), const),
            _resident((d_model, d_in), lambda i: (0, 0)),
            _resident((None, 1, d_attn), const),
            _resident((None, 1, d_kv), const),
            cast_in,
        ],
        out_specs=[pl.BlockSpec((tm, width), row), cast_out],
        out_shape=[jax.ShapeDtypeStruct((t, width), BF16), cast_shape],
        compiler_params=_compiler_params(1),
        name="inproj",
    )(x2d, norm_g, w_in_bf16, qn_tiled, kn_tiled, w_out)


def _mixout_kernel(sinks_ref, x_ref, act_ref, prev_ref, ws_ref, bfull_ref, wout_ref, *rest,
                   layer, n_q_heads, n_gmlp_heads, tb, d_attn, convert_next_w_in):
    if convert_next_w_in:
        win_slice_ref, o_ref, win_bf16_ref, bias, wcat, ones = rest
        win_bf16_ref[...] = win_slice_ref[...].astype(BF16)
    else:
        o_ref, bias, wcat, ones = rest
    n_blocks = tb // WINDOW
    d_kv = N_KV_HEADS * HEAD_DIM
    d_gmlp = n_gmlp_heads * HEAD_DIM
    lay = _act_layout(d_attn, d_kv, d_gmlp)
    first_tile = (pl.program_id(0) == 0) & (pl.program_id(1) == 0)

    @pl.when(first_tile)
    def _init_tables():
        qpos = lax.broadcasted_iota(jnp.int32, (WINDOW, 2 * WINDOW), 0) + WINDOW
        kpos = lax.broadcasted_iota(jnp.int32, (WINDOW, 2 * WINDOW), 1)
        dist = qpos - kpos
        in_window = (dist >= 0) & (dist < WINDOW)
        dist_f = dist.astype(F32)
        for j in range(n_q_heads):
            slope = np.float32(2.0 ** (-8.0 * (j + 1) / n_q_heads))
            alibi = (-slope * LOG2_E) * dist_f
            rows = slice((j // 2 % 2) * WINDOW, (j // 2 % 2 + 1) * WINDOW)
            lanes = slice((j % 2) * 2 * WINDOW, (j % 2 + 1) * 2 * WINDOW)
            bias[1, j // 4, rows, lanes] = jnp.where(in_window, alibi, -jnp.inf)
            bias[0, j // 4, rows, lanes] = jnp.where(in_window & (kpos >= WINDOW), alibi, -jnp.inf)
        row = lax.broadcasted_iota(jnp.int32, (CHUNK, CHUNK), 0)
        col = lax.broadcasted_iota(jnp.int32, (CHUNK, CHUNK), 1)
        for hh in range(n_gmlp_heads):
            wcat[hh // 2, :, (hh % 2) * CHUNK:(hh % 2 + 1) * CHUNK] = (
                jnp.where(row >= col, ws_ref[hh], 0.0).astype(BF16))
        r = lax.broadcasted_iota(jnp.int32, (4 * WINDOW, LANES), 0)
        c = lax.broadcasted_iota(jnp.int32, (4 * WINDOW, LANES), 1)
        ones[...] = jnp.where((r < 2 * WINDOW) == (c < HEAD_DIM), 1.0, 0.0).astype(BF16)

    low_half = _low_half(WINDOW)
    has_prev0 = jnp.where(pl.program_id(1) == 0, 0, 1)
    zeros_kt = jnp.zeros((HEAD_DIM, 2 * WINDOW), BF16)

    def band(jb, rows_in_block, lanes):
        prev = (prev_ref[rows_in_block, lanes] if jb == 0 else
                act_ref[(jb - 1) * WINDOW + rows_in_block.start:
                        (jb - 1) * WINDOW + rows_in_block.stop, lanes])
        own = act_ref[jb * WINDOW + rows_in_block.start:jb * WINDOW + rows_in_block.stop, lanes]
        return [prev, own]

    all_rows = slice(0, WINDOW)
    mix_vals = {}

    def block_rows(jb):
        return slice(jb * WINDOW, (jb + 1) * WINDOW)

    def q_cols(hk):
        return [slice((2 * hk + pr) * LANES, (2 * hk + pr + 1) * LANES) for pr in range(2)]

    def scores(jb, hk):
        feat_rows = slice((hk % 2) * HEAD_DIM, (hk % 2 + 1) * HEAD_DIM)
        kt_lanes = slice(lay.kt + (hk // 2) * LANES, lay.kt + (hk // 2 + 1) * LANES)
        kband = jnp.concatenate(band(jb, feat_rows, kt_lanes), axis=1)
        krhs = jnp.concatenate([jnp.concatenate([kband, zeros_kt], axis=1),
                                jnp.concatenate([zeros_kt, kband], axis=1)], axis=0)
        q2 = jnp.concatenate([act_ref[block_rows(jb), lay.q + cl.start:lay.q + cl.stop]
                              for cl in q_cols(hk)], axis=0)
        s = jnp.dot(q2, krhs, preferred_element_type=F32)
        return s + (bias[has_prev0, hk] if jb == 0 else bias[1, hk])

    def softmax_numerators(hk, s):
        probs, sink_terms = [], []
        for pr in range(2):
            sink_a = sinks_ref[layer, 4 * hk + 2 * pr] * LOG2_E
            sink_b = sinks_ref[layer, 4 * hk + 2 * pr + 1] * LOG2_E
            s_a = s[pr * WINDOW:(pr + 1) * WINDOW, :2 * WINDOW]
            s_b = s[pr * WINDOW:(pr + 1) * WINDOW, 2 * WINDOW:]
            m_a = jnp.maximum(jnp.max(s_a, axis=-1, keepdims=True), sink_a)
            m_b = jnp.maximum(jnp.max(s_b, axis=-1, keepdims=True), sink_b)
            probs.append(jnp.concatenate([jnp.exp2(s_a - m_a), jnp.exp2(s_b - m_b)],
                                         axis=1).astype(BF16))
            sink_terms.append(jnp.where(low_half, jnp.exp2(sink_a - m_a),
                                        jnp.exp2(sink_b - m_b)))
        return jnp.concatenate(probs, axis=0), sink_terms

    def weighted_values(jb, hk, probs):
        vx_lanes = slice(lay.vx + hk * LANES, lay.vx + (hk + 1) * LANES)
        vy_lanes = slice(lay.vy + hk * LANES, lay.vy + (hk + 1) * LANES)
        vstack = jnp.concatenate(band(jb, all_rows, vx_lanes) + band(jb, all_rows, vy_lanes),
                                 axis=0)
        vrhs = jnp.concatenate([vstack, ones[...]], axis=1)
        return jnp.dot(probs, vrhs, preferred_element_type=F32)

    def normalize_and_gate(jb, hk, o, sink_terms):
        rows = block_rows(jb)
        for pr, cl in enumerate(q_cols(hk)):
            o_pr = o[pr * WINDOW:(pr + 1) * WINDOW, :]
            attn = o_pr[:, :LANES] / (o_pr[:, LANES:] + sink_terms[pr])
            ga = act_ref[rows, lay.ga + cl.start:lay.ga + cl.stop]
            mix_vals[(jb, cl.start // LANES)] = attn.astype(BF16) * ga

    def spatial_gating_pair(blocks, jp):
        cl = slice(jp * LANES, (jp + 1) * LANES)
        tiles = []
        for jb in blocks:
            vpair = act_ref[block_rows(jb), lay.vs + cl.start:lay.vs + cl.stop]
            zero = jnp.zeros_like(vpair)
            tiles.append(jnp.concatenate([jnp.where(low_half, vpair, zero),
                                          jnp.where(low_half, zero, vpair)], axis=0))
        mixed = jnp.dot(wcat[jp], jnp.concatenate(tiles, axis=1),
                        preferred_element_type=F32)
        for n, jb in enumerate(blocks):
            rows = block_rows(jb)
            mixed_jb = mixed[:, n * LANES:(n + 1) * LANES] + bfull_ref[:, cl]
            u = act_ref[rows, lay.u + cl.start:lay.u + cl.stop]
            gb = act_ref[rows, lay.gb + cl.start:lay.gb + cl.stop]
            mix_vals[(jb, (d_attn + cl.start) // LANES)] = mixed_jb.astype(BF16) * (u * gb)

    def out_rows(jo):
        rows = slice(jo * OUT_ROWS, (jo + 1) * OUT_ROWS)
        group = range(jo * OUT_ROWS // WINDOW, (jo + 1) * OUT_ROWS // WINDOW)
        mix_value = jnp.concatenate(
            [jnp.concatenate([mix_vals[(jb, c)] for jb in group], axis=0)
             for c in range(wout_ref.shape[0] // LANES)], axis=1)
        o_ref[rows, :] = x_ref[rows, :] + jnp.dot(mix_value, wout_ref[...],
                                                  preferred_element_type=F32)

    blocks_per_out = OUT_ROWS // WINDOW
    for jo in range(n_blocks // blocks_per_out):
        blocks = range(jo * blocks_per_out, (jo + 1) * blocks_per_out)
        items = [(jb, hk) for jb in blocks for hk in range(N_KV_HEADS)]
        gating_pairs = list(range(n_gmlp_heads // 2))
        s_queue = [scores(*item) for item in items[:SCORES_AHEAD]]
        pending = None
        for idx, (jb, hk) in enumerate(items):
            if idx + SCORES_AHEAD < len(items):
                s_queue.append(scores(*items[idx + SCORES_AHEAD]))
            probs, sink_terms = softmax_numerators(hk, s_queue.pop(0))
            o = weighted_values(jb, hk, probs)
            if pending is not None:
                normalize_and_gate(*pending)
            pending = (jb, hk, o, sink_terms)
            if idx >= len(items) - len(gating_pairs) and gating_pairs:
                spatial_gating_pair(blocks, gating_pairs.pop(0))
        normalize_and_gate(*pending)
        for jp in gating_pairs:
            spatial_gating_pair(blocks, jp)
        out_rows(jo)


def _mixout(layer, sinks, x2d, act, w_s, b_full, w_out_bf16, w_in, *, batch, seq, d_attn, d_gmlp):
    t, d_model = x2d.shape
    convert_next_w_in = layer + 1 < w_in.shape[0]
    d_kv = N_KV_HEADS * HEAD_DIM
    lay = _act_layout(d_attn, d_kv, d_gmlp)
    n_q_heads = d_attn // HEAD_DIM
    n_gmlp_heads = d_gmlp // HEAD_DIM
    tb = TOKENS_PER_STEP
    tiles = seq // tb
    blocks_per_tile = tb // WINDOW
    row = lambda b, i: (b * tiles + i, 0)
    prev = lambda b, i: (b * (seq // WINDOW) + jnp.maximum(i * blocks_per_tile - 1, 0), 0)
    in_specs = [
        pl.BlockSpec(memory_space=pltpu.SMEM),
        pl.BlockSpec((tb, d_model), row),
        pl.BlockSpec((tb, lay.width), row),
        pl.BlockSpec((WINDOW, lay.band_width), prev),
        _resident((None, n_gmlp_heads, CHUNK, CHUNK), lambda b, i: (layer, 0, 0, 0)),
        _resident((None, CHUNK, d_gmlp), lambda b, i: (layer, 0, 0)),
        _resident((d_attn + d_gmlp, d_model), lambda b, i: (0, 0)),
    ]
    operands = [sinks, x2d, act, act, w_s, b_full, w_out_bf16]
    out_specs = [pl.BlockSpec((tb, d_model), row)]
    out_shape = [jax.ShapeDtypeStruct((t, d_model), F32)]
    if convert_next_w_in:
        cast_in, cast_out, cast_shape = _cast_slice_specs(
            w_in, layer + 1, batch * tiles, lambda b, i: b * tiles + i)
        in_specs.append(cast_in)
        operands.append(w_in)
        out_specs.append(cast_out)
        out_shape.append(cast_shape)
    outs = pl.pallas_call(
        functools.partial(_mixout_kernel, layer=layer, n_q_heads=n_q_heads,
                          n_gmlp_heads=n_gmlp_heads, tb=tb, d_attn=d_attn,
                          convert_next_w_in=convert_next_w_in),
        grid=(batch, tiles),
        in_specs=in_specs,
        out_specs=out_specs,
        out_shape=out_shape,
        scratch_shapes=[
            pltpu.VMEM((2, N_KV_HEADS, 2 * WINDOW, 4 * WINDOW), F32),
            pltpu.VMEM((n_gmlp_heads // 2, CHUNK, 2 * CHUNK), BF16),
            pltpu.VMEM((4 * WINDOW, LANES), BF16),
        ],
        compiler_params=_compiler_params(2),
        name="mixout",
    )(*operands)
    return outs if convert_next_w_in else (outs[0], None)


def kernel(x, norm_g, w_in, q_norm, k_norm, sinks, w_s, b_s, w_out):
    batch, seq, d_model = x.shape
    depth = norm_g.shape[0]
    n_q_heads = sinks.shape[1]
    n_gmlp_heads = w_s.shape[1]
    d_attn = n_q_heads * HEAD_DIM
    d_gmlp = n_gmlp_heads * HEAD_DIM
    d_kv = N_KV_HEADS * HEAD_DIM
    assert w_in.shape[2] == 2 * d_attn + 2 * d_kv + 3 * d_gmlp
    assert seq % TOKENS_PER_STEP == 0 and q_norm.shape[1] == HEAD_DIM
    assert n_q_heads == 4 * N_KV_HEADS and TOKENS_PER_STEP % OUT_ROWS == 0

    norm_g3 = norm_g.reshape(depth, 1, d_model)
    qn_tiled = jnp.tile(q_norm, (1, n_q_heads)).reshape(depth, 1, d_attn)
    kn_tiled = jnp.tile(k_norm, (1, N_KV_HEADS)).reshape(depth, 1, d_kv)
    b_full = jnp.repeat(jnp.swapaxes(b_s, 1, 2), HEAD_DIM, axis=2)

    x2d = x.reshape(batch * seq, d_model)
    w_in_bf16 = w_in[0].astype(BF16)
    for l in range(depth):
        act, w_out_bf16 = _inproj(l, x2d, norm_g3, w_in_bf16, qn_tiled, kn_tiled, w_out,
                                  d_attn=d_attn, d_kv=d_kv, d_gmlp=d_gmlp)
        x2d, w_in_bf16 = _mixout(l, sinks, x2d, act, w_s, b_full, w_out_bf16, w_in,
                                 batch=batch, seq=seq, d_attn=d_attn, d_gmlp=d_gmlp)
    return x2d.reshape(batch, seq, d_model)
```

```python
import collections
import functools

import jax
import jax.numpy as jnp
import numpy as np
from jax import lax
from jax.experimental import pallas as pl
from jax.experimental.pallas import tpu as pltpu

HEAD_DIM = 64
N_KV_HEADS = 4
WINDOW = 128
CHUNK = 128
RMS_EPS = 1e-6
LANES = 128
VMEM_LIMIT_BYTES = 56 * 1024 * 1024
TOKENS_PER_STEP = 512
OUT_ROWS = 512
SCORES_AHEAD = 2
LOG2_E = float(np.log2(np.e))

F32 = jnp.float32
BF16 = jnp.bfloat16

ActLayout = collections.namedtuple("ActLayout", "kt vx vy q ga u vs gb width band_width")


def _act_layout(d_attn, d_kv, d_gmlp):
    kt, vx = 0, d_kv
    vy = vx + 2 * d_kv
    q = vy + 2 * d_kv
    ga = q + d_attn
    u = ga + d_attn
    vs = u + d_gmlp
    gb = vs + d_gmlp
    return ActLayout(kt, vx, vy, q, ga, u, vs, gb, width=gb + d_gmlp, band_width=q)


def _compiler_params(n_grid_dims):
    return pltpu.CompilerParams(
        dimension_semantics=("arbitrary",) * n_grid_dims,
        vmem_limit_bytes=VMEM_LIMIT_BYTES,
    )


def _resident(block_shape, index_map):
    return pl.BlockSpec(block_shape, index_map, pipeline_mode=pl.Buffered(1))


def _cast_slice_specs(weight, layer, n_steps, step_index):
    _, rows, cols = weight.shape
    slice_rows = rows // n_steps
    assert slice_rows * n_steps == rows
    in_spec = pl.BlockSpec((None, slice_rows, cols), lambda *g: (layer, step_index(*g), 0))
    out_spec = pl.BlockSpec((slice_rows, cols), lambda *g: (step_index(*g), 0))
    return in_spec, out_spec, jax.ShapeDtypeStruct((rows, cols), BF16)


def _low_half(rows):
    return lax.broadcasted_iota(jnp.int32, (rows, LANES), 1) < HEAD_DIM


def _head_rmsnorm(x, gain_tiled):
    rows, width = x.shape
    low_half = _low_half(rows)
    cols = []
    for c in range(width // LANES):
        blk = x[:, c * LANES:(c + 1) * LANES]
        sq = blk * blk
        ss_lo = jnp.sum(jnp.where(low_half, sq, 0.0), axis=-1, keepdims=True)
        ss_hi = jnp.sum(jnp.where(low_half, 0.0, sq), axis=-1, keepdims=True)
        r_lo = lax.rsqrt(ss_lo * (1.0 / HEAD_DIM) + RMS_EPS)
        r_hi = lax.rsqrt(ss_hi * (1.0 / HEAD_DIM) + RMS_EPS)
        cols.append(blk * jnp.where(low_half, r_lo, r_hi))
    return jnp.concatenate(cols, axis=-1) * gain_tiled


def _split_heads(x):
    rows, width = x.shape
    low_half = _low_half(rows)
    lo, hi = [], []
    for c in range(width // LANES):
        col = x[:, c * LANES:(c + 1) * LANES]
        swapped = pltpu.roll(col, HEAD_DIM, axis=1)
        lo += [jnp.where(low_half, col, 0.0), jnp.where(low_half, swapped, 0.0)]
        hi += [jnp.where(low_half, 0.0, swapped), jnp.where(low_half, 0.0, col)]
    return jnp.concatenate(lo, axis=-1), jnp.concatenate(hi, axis=-1)


def _gelu_exact(x):
    return 0.5 * x * (1.0 + lax.erf(x * np.float32(1.0 / np.sqrt(2.0))))


def _silu(x):
    half = 0.5 * x
    return half + half * jnp.tanh(half)


def _inproj_kernel(x_ref, g_ref, w_ref, qn_ref, kn_ref, wout_slice_ref,
                   act_ref, wout_bf16_ref, *, d_attn, d_kv, d_gmlp):
    lay = _act_layout(d_attn, d_kv, d_gmlp)
    wout_bf16_ref[...] = wout_slice_ref[...].astype(BF16)
    x = x_ref[...]
    xg = (x * g_ref[...]).astype(BF16)
    r = lax.rsqrt(jnp.mean(x * x, axis=-1, keepdims=True) + RMS_EPS)

    def proj(lo, width):
        return r * jnp.dot(xg, w_ref[:, lo:lo + width], preferred_element_type=F32)

    def put(off, value):
        act_ref[:, off:off + value.shape[1]] = value.astype(BF16)

    q_off, k_off, v_off = 0, d_attn, d_attn + d_kv
    ga_off = v_off + d_kv
    u_off, vs_off, gb_off = ga_off + d_attn, ga_off + d_attn + d_gmlp, ga_off + d_attn + 2 * d_gmlp

    q = _head_rmsnorm(proj(q_off, d_attn), qn_ref[...])
    put(lay.q, q * (HEAD_DIM ** -0.5 * LOG2_E))
    k = _head_rmsnorm(proj(k_off, d_kv), kn_ref[...])
    for jb in range(k.shape[0] // WINDOW):
        for c in range(d_kv // LANES):
            tile = k[jb * WINDOW:(jb + 1) * WINDOW, c * LANES:(c + 1) * LANES]
            act_ref[jb * WINDOW:(jb + 1) * WINDOW,
                    lay.kt + c * LANES:lay.kt + (c + 1) * LANES] = tile.T.astype(BF16)
    vx, vy = _split_heads(proj(v_off, d_kv))
    put(lay.vx, vx)
    put(lay.vy, vy)
    put(lay.u, _gelu_exact(proj(u_off, d_gmlp)))
    put(lay.vs, _gelu_exact(proj(vs_off, d_gmlp)))
    put(lay.ga, _silu(proj(ga_off, d_attn)))
    put(lay.gb, _silu(proj(gb_off, d_gmlp)))


def _inproj(layer, x2d, norm_g, w_in_bf16, qn_tiled, kn_tiled, w_out, *, d_attn, d_kv, d_gmlp):
    t, d_model = x2d.shape
    d_in = w_in_bf16.shape[1]
    tm = TOKENS_PER_STEP
    width = _act_layout(d_attn, d_kv, d_gmlp).width
    row = lambda i: (i, 0)
    const = lambda i: (layer, 0, 0)
    cast_in, cast_out, cast_shape = _cast_slice_specs(w_out, layer, t // tm, lambda i: i)
    return pl.pallas_call(
        functools.partial(_inproj_kernel, d_attn=d_attn, d_kv=d_kv, d_gmlp=d_gmlp),
        grid=(t // tm,),
        in_specs=[
            pl.BlockSpec((tm, d_model), row),
            _resident((None, 1, d_model), const),
            _resident((d_model, d_in), lambda i: (0, 0)),
            _resident((None, 1, d_attn), const),
            _resident((None, 1, d_kv), const),
            cast_in,
        ],
        out_specs=[pl.BlockSpec((tm, width), row), cast_out],
        out_shape=[jax.ShapeDtypeStruct((t, width), BF16), cast_shape],
        compiler_params=_compiler_params(1),
        name="inproj",
    )(x2d, norm_g, w_in_bf16, qn_tiled, kn_tiled, w_out)


def _mixout_kernel(sinks_ref, x_ref, act_ref, prev_ref, ws_ref, bfull_ref, wout_ref, *rest,
                   layer, n_q_heads, n_gmlp_heads, tb, d_attn, convert_next_w_in):
    if convert_next_w_in:
        win_slice_ref, o_ref, win_bf16_ref, bias, wcat, ones = rest
        win_bf16_ref[...] = win_slice_ref[...].astype(BF16)
    else:
        o_ref, bias, wcat, ones = rest
    n_blocks = tb // WINDOW
    d_kv = N_KV_HEADS * HEAD_DIM
    d_gmlp = n_gmlp_heads * HEAD_DIM
    lay = _act_layout(d_attn, d_kv, d_gmlp)
    first_tile = (pl.program_id(0) == 0) & (pl.program_id(1) == 0)

    @pl.when(first_tile)
    def _init_tables():
        qpos = lax.broadcasted_iota(jnp.int32, (WINDOW, 2 * WINDOW), 0) + WINDOW
        kpos = lax.broadcasted_iota(jnp.int32, (WINDOW, 2 * WINDOW), 1)
        dist = qpos - kpos
        in_window = (dist >= 0) & (dist < WINDOW)
        dist_f = dist.astype(F32)
        for j in range(n_q_heads):
            slope = np.float32(2.0 ** (-8.0 * (j + 1) / n_q_heads))
            alibi = (-slope * LOG2_E) * dist_f
            sink = sinks_ref[layer, j] * LOG2_E
            rows = slice((j // 2 % 2) * WINDOW, (j // 2 % 2 + 1) * WINDOW)
            lanes = slice((j % 2) * 2 * WINDOW, (j % 2 + 1) * 2 * WINDOW)
            bias[1, j // 4, rows, lanes] = jnp.where(
                kpos == 0, sink, jnp.where(in_window, alibi, -jnp.inf))
            bias[0, j // 4, rows, lanes] = jnp.where(
                kpos == 0, sink, jnp.where(in_window & (kpos >= WINDOW), alibi, -jnp.inf))
        row = lax.broadcasted_iota(jnp.int32, (CHUNK, CHUNK), 0)
        col = lax.broadcasted_iota(jnp.int32, (CHUNK, CHUNK), 1)
        for hh in range(n_gmlp_heads):
            wcat[hh // 2, :, (hh % 2) * CHUNK:(hh % 2 + 1) * CHUNK] = (
                jnp.where(row >= col, ws_ref[hh], 0.0).astype(BF16))
        r = lax.broadcasted_iota(jnp.int32, (4 * WINDOW, LANES), 0)
        c = lax.broadcasted_iota(jnp.int32, (4 * WINDOW, LANES), 1)
        ones[...] = jnp.where((r < 2 * WINDOW) == (c < HEAD_DIM), 1.0, 0.0).astype(BF16)

    low_half = _low_half(WINDOW)
    has_prev0 = jnp.where(pl.program_id(1) == 0, 0, 1)
    zeros_kt = jnp.zeros((HEAD_DIM, 2 * WINDOW), BF16)

    def band(jb, rows_in_block, lanes, sink_slot_axis):
        prev = (prev_ref[rows_in_block, lanes] if jb == 0 else
                act_ref[(jb - 1) * WINDOW + rows_in_block.start:
                        (jb - 1) * WINDOW + rows_in_block.stop, lanes])
        position = lax.broadcasted_iota(jnp.int32, prev.shape, sink_slot_axis)
        prev = jnp.where(position == 0, jnp.zeros_like(prev), prev)
        own = act_ref[jb * WINDOW + rows_in_block.start:jb * WINDOW + rows_in_block.stop, lanes]
        return [prev, own]

    all_rows = slice(0, WINDOW)
    mix_vals = {}

    def block_rows(jb):
        return slice(jb * WINDOW, (jb + 1) * WINDOW)

    def q_cols(hk):
        return [slice((2 * hk + pr) * LANES, (2 * hk + pr + 1) * LANES) for pr in range(2)]

    def scores(jb, hk):
        feat_rows = slice((hk % 2) * HEAD_DIM, (hk % 2 + 1) * HEAD_DIM)
        kt_lanes = slice(lay.kt + (hk // 2) * LANES, lay.kt + (hk // 2 + 1) * LANES)
        kband = jnp.concatenate(band(jb, feat_rows, kt_lanes, 1), axis=1)
        krhs = jnp.concatenate([jnp.concatenate([kband, zeros_kt], axis=1),
                                jnp.concatenate([zeros_kt, kband], axis=1)], axis=0)
        q2 = jnp.concatenate([act_ref[block_rows(jb), lay.q + cl.start:lay.q + cl.stop]
                              for cl in q_cols(hk)], axis=0)
        s = jnp.dot(q2, krhs, preferred_element_type=F32)
        return s + (bias[has_prev0, hk] if jb == 0 else bias[1, hk])

    def softmax_numerators(s):
        probs = []
        for pr in range(2):
            s_a = s[pr * WINDOW:(pr + 1) * WINDOW, :2 * WINDOW]
            s_b = s[pr * WINDOW:(pr + 1) * WINDOW, 2 * WINDOW:]
            m_a = jnp.max(s_a, axis=-1, keepdims=True)
            m_b = jnp.max(s_b, axis=-1, keepdims=True)
            probs.append(jnp.concatenate([jnp.exp2(s_a - m_a), jnp.exp2(s_b - m_b)],
                                         axis=1).astype(BF16))
        return jnp.concatenate(probs, axis=0)

    def weighted_values(jb, hk, probs):
        vx_lanes = slice(lay.vx + hk * LANES, lay.vx + (hk + 1) * LANES)
        vy_lanes = slice(lay.vy + hk * LANES, lay.vy + (hk + 1) * LANES)
        vstack = jnp.concatenate(band(jb, all_rows, vx_lanes, 0) + band(jb, all_rows, vy_lanes, 0),
                                 axis=0)
        vrhs = jnp.concatenate([vstack, ones[...]], axis=1)
        return jnp.dot(probs, vrhs, preferred_element_type=F32)

    def normalize_and_gate(jb, hk, o):
        rows = block_rows(jb)
        for pr, cl in enumerate(q_cols(hk)):
            o_pr = o[pr * WINDOW:(pr + 1) * WINDOW, :]
            attn = o_pr[:, :LANES] / o_pr[:, LANES:]
            ga = act_ref[rows, lay.ga + cl.start:lay.ga + cl.stop]
            mix_vals[(jb, cl.start // LANES)] = attn.astype(BF16) * ga

    def spatial_gating_pair(blocks, jp):
        cl = slice(jp * LANES, (jp + 1) * LANES)
        tiles = []
        for jb in blocks:
            vpair = act_ref[block_rows(jb), lay.vs + cl.start:lay.vs + cl.stop]
            zero = jnp.zeros_like(vpair)
            tiles.append(jnp.concatenate([jnp.where(low_half, vpair, zero),
                                          jnp.where(low_half, zero, vpair)], axis=0))
        mixed = jnp.dot(wcat[jp], jnp.concatenate(tiles, axis=1),
                        preferred_element_type=F32)
        for n, jb in enumerate(blocks):
            rows = block_rows(jb)
            mixed_jb = mixed[:, n * LANES:(n + 1) * LANES] + bfull_ref[:, cl]
            u = act_ref[rows, lay.u + cl.start:lay.u + cl.stop]
            gb = act_ref[rows, lay.gb + cl.start:lay.gb + cl.stop]
            mix_vals[(jb, (d_attn + cl.start) // LANES)] = mixed_jb.astype(BF16) * (u * gb)

    def out_rows(jo):
        rows = slice(jo * OUT_ROWS, (jo + 1) * OUT_ROWS)
        group = range(jo * OUT_ROWS // WINDOW, (jo + 1) * OUT_ROWS // WINDOW)
        mix_value = jnp.concatenate(
            [jnp.concatenate([mix_vals[(jb, c)] for jb in group], axis=0)
             for c in range(wout_ref.shape[0] // LANES)], axis=1)
        o_ref[rows, :] = x_ref[rows, :] + jnp.dot(mix_value, wout_ref[...],
                                                  preferred_element_type=F32)

    blocks_per_out = OUT_ROWS // WINDOW
    for jo in range(n_blocks // blocks_per_out):
        blocks = range(jo * blocks_per_out, (jo + 1) * blocks_per_out)
        items = [(jb, hk) for jb in blocks for hk in range(N_KV_HEADS)]
        gating_pairs = list(range(n_gmlp_heads // 2))
        s_queue = [scores(*item) for item in items[:SCORES_AHEAD]]
        pending = None
        for idx, (jb, hk) in enumerate(items):
            if idx + SCORES_AHEAD < len(items):
                s_queue.append(scores(*items[idx + SCORES_AHEAD]))
            o = weighted_values(jb, hk, softmax_numerators(s_queue.pop(0)))
            if pending is not None:
                normalize_and_gate(*pending)
            pending = (jb, hk, o)
            if idx >= len(items) - len(gating_pairs) and gating_pairs:
                spatial_gating_pair(blocks, gating_pairs.pop(0))
        normalize_and_gate(*pending)
        for jp in gating_pairs:
            spatial_gating_pair(blocks, jp)
        out_rows(jo)


def _mixout(layer, sinks, x2d, act, w_s, b_full, w_out_bf16, w_in, *, batch, seq, d_attn, d_gmlp):
    t, d_model = x2d.shape
    convert_next_w_in = layer + 1 < w_in.shape[0]
    d_kv = N_KV_HEADS * HEAD_DIM
    lay = _act_layout(d_attn, d_kv, d_gmlp)
    n_q_heads = d_attn // HEAD_DIM
    n_gmlp_heads = d_gmlp // HEAD_DIM
    tb = TOKENS_PER_STEP
    tiles = seq // tb
    blocks_per_tile = tb // WINDOW
    row = lambda b, i: (b * tiles + i, 0)
    prev = lambda b, i: (b * (seq // WINDOW) + jnp.maximum(i * blocks_per_tile - 1, 0), 0)
    in_specs = [
        pl.BlockSpec(memory_space=pltpu.SMEM),
        pl.BlockSpec((tb, d_model), row),
        pl.BlockSpec((tb, lay.width), row),
        pl.BlockSpec((WINDOW, lay.band_width), prev),
        _resident((None, n_gmlp_heads, CHUNK, CHUNK), lambda b, i: (layer, 0, 0, 0)),
        _resident((None, CHUNK, d_gmlp), lambda b, i: (layer, 0, 0)),
        _resident((d_attn + d_gmlp, d_model), lambda b, i: (0, 0)),
    ]
    operands = [sinks, x2d, act, act, w_s, b_full, w_out_bf16]
    out_specs = [pl.BlockSpec((tb, d_model), row)]
    out_shape = [jax.ShapeDtypeStruct((t, d_model), F32)]
    if convert_next_w_in:
        cast_in, cast_out, cast_shape = _cast_slice_specs(
            w_in, layer + 1, batch * tiles, lambda b, i: b * tiles + i)
        in_specs.append(cast_in)
        operands.append(w_in)
        out_specs.append(cast_out)
        out_shape.append(cast_shape)
    outs = pl.pallas_call(
        functools.partial(_mixout_kernel, layer=layer, n_q_heads=n_q_heads,
                          n_gmlp_heads=n_gmlp_heads, tb=tb, d_attn=d_attn,
                          convert_next_w_in=convert_next_w_in),
        grid=(batch, tiles),
        in_specs=in_specs,
        out_specs=out_specs,
        out_shape=out_shape,
        scratch_shapes=[
            pltpu.VMEM((2, N_KV_HEADS, 2 * WINDOW, 4 * WINDOW), F32),
            pltpu.VMEM((n_gmlp_heads // 2, CHUNK, 2 * CHUNK), BF16),
            pltpu.VMEM((4 * WINDOW, LANES), BF16),
        ],
        compiler_params=_compiler_params(2),
        name="mixout",
    )(*operands)
    return outs if convert_next_w_in else (outs[0], None)


def kernel(x, norm_g, w_in, q_norm, k_norm, sinks, w_s, b_s, w_out):
    batch, seq, d_model = x.shape
    depth = norm_g.shape[0]
    n_q_heads = sinks.shape[1]
    n_gmlp_heads = w_s.shape[1]
    d_attn = n_q_heads * HEAD_DIM
    d_gmlp = n_gmlp_heads * HEAD_DIM
    d_kv = N_KV_HEADS * HEAD_DIM
    assert w_in.shape[2] == 2 * d_attn + 2 * d_kv + 3 * d_gmlp
    assert seq % TOKENS_PER_STEP == 0 and q_norm.shape[1] == HEAD_DIM
    assert n_q_heads == 4 * N_KV_HEADS and TOKENS_PER_STEP % OUT_ROWS == 0

    norm_g3 = norm_g.reshape(depth, 1, d_model)
    qn_tiled = jnp.tile(q_norm, (1, n_q_heads)).reshape(depth, 1, d_attn)
    kn_tiled = jnp.tile(k_norm, (1, N_KV_HEADS)).reshape(depth, 1, d_kv)
    b_full = jnp.repeat(jnp.swapaxes(b_s, 1, 2), HEAD_DIM, axis=2)

    x2d = x.reshape(batch * seq, d_model)
    w_in_bf16 = w_in[0].astype(BF16)
    for l in range(depth):
        act, w_out_bf16 = _inproj(l, x2d, norm_g3, w_in_bf16, qn_tiled, kn_tiled, w_out,
                                  d_attn=d_attn, d_kv=d_kv, d_gmlp=d_gmlp)
        x2d, w_in_bf16 = _mixout(l, sinks, x2d, act, w_s, b_full, w_out_bf16, w_in,
                                 batch=batch, seq=seq, d_attn=d_attn, d_gmlp=d_gmlp)
    return x2d.reshape(batch, seq, d_model)
```

```python
import collections
import functools

import jax
import jax.numpy as jnp
import numpy as np
from jax import lax
from jax.experimental import pallas as pl
from jax.experimental.pallas import tpu as pltpu

HEAD_DIM = 64
N_KV_HEADS = 4
WINDOW = 128
CHUNK = 128
RMS_EPS = 1e-6
LANES = 128
VMEM_LIMIT_BYTES = 56 * 1024 * 1024
TOKENS_PER_STEP = 512
OUT_ROWS = 512
SCORES_AHEAD = 1
LOG2_E = float(np.log2(np.e))

F32 = jnp.float32
BF16 = jnp.bfloat16

ActLayout = collections.namedtuple("ActLayout", "kt vx vy q ga u vs gb width band_width")


def _act_layout(d_attn, d_kv, d_gmlp):
    kt, vx = 0, d_kv
    vy = vx + 2 * d_kv
    q = vy + 2 * d_kv
    ga = q + d_attn
    u = ga + d_attn
    vs = u + d_gmlp
    gb = vs + d_gmlp
    return ActLayout(kt, vx, vy, q, ga, u, vs, gb, width=gb + d_gmlp, band_width=q)


def _compiler_params(n_grid_dims):
    return pltpu.CompilerParams(
        dimension_semantics=("arbitrary",) * n_grid_dims,
        vmem_limit_bytes=VMEM_LIMIT_BYTES,
    )


def _resident(block_shape, index_map):
    return pl.BlockSpec(block_shape, index_map, pipeline_mode=pl.Buffered(1))


def _cast_slice_specs(weight, layer, n_steps, step_index):
    _, rows, cols = weight.shape
    slice_rows = rows // n_steps
    assert slice_rows * n_steps == rows
    in_spec = pl.BlockSpec((None, slice_rows, cols), lambda *g: (layer, step_index(*g), 0))
    out_spec = pl.BlockSpec((slice_rows, cols), lambda *g: (step_index(*g), 0))
    return in_spec, out_spec, jax.ShapeDtypeStruct((rows, cols), BF16)


def _low_half(rows):
    return lax.broadcasted_iota(jnp.int32, (rows, LANES), 1) < HEAD_DIM


def _head_rmsnorm(x, gain_tiled):
    rows, width = x.shape
    low_half = _low_half(rows)
    cols = []
    for c in range(width // LANES):
        blk = x[:, c * LANES:(c + 1) * LANES]
        sq = blk * blk
        ss_lo = jnp.sum(jnp.where(low_half, sq, 0.0), axis=-1, keepdims=True)
        ss_hi = jnp.sum(jnp.where(low_half, 0.0, sq), axis=-1, keepdims=True)
        r_lo = lax.rsqrt(ss_lo * (1.0 / HEAD_DIM) + RMS_EPS)
        r_hi = lax.rsqrt(ss_hi * (1.0 / HEAD_DIM) + RMS_EPS)
        cols.append(blk * jnp.where(low_half, r_lo, r_hi))
    return jnp.concatenate(cols, axis=-1) * gain_tiled


def _split_heads(x):
    rows, width = x.shape
    low_half = _low_half(rows)
    lo, hi = [], []
    for c in range(width // LANES):
        col = x[:, c * LANES:(c + 1) * LANES]
        swapped = pltpu.roll(col, HEAD_DIM, axis=1)
        lo += [jnp.where(low_half, col, 0.0), jnp.where(low_half, swapped, 0.0)]
        hi += [jnp.where(low_half, 0.0, swapped), jnp.where(low_half, 0.0, col)]
    return jnp.concatenate(lo, axis=-1), jnp.concatenate(hi, axis=-1)


def _gelu_exact(x):
    return 0.5 * x * (1.0 + lax.erf(x * np.float32(1.0 / np.sqrt(2.0))))


def _silu(x):
    half = 0.5 * x
    return half + half * jnp.tanh(half)


def _inproj_kernel(x_ref, g_ref, w_ref, qn_ref, kn_ref, wout_slice_ref,
                   act_ref, wout_bf16_ref, *, d_attn, d_kv, d_gmlp):
    lay = _act_layout(d_attn, d_kv, d_gmlp)
    wout_bf16_ref[...] = wout_slice_ref[...].astype(BF16)
    x = x_ref[...]
    xg = (x * g_ref[...]).astype(BF16)
    r = lax.rsqrt(jnp.mean(x * x, axis=-1, keepdims=True) + RMS_EPS)

    def proj(lo, width):
        return r * jnp.dot(xg, w_ref[:, lo:lo + width], preferred_element_type=F32)

    def put(off, value):
        act_ref[:, off:off + value.shape[1]] = value.astype(BF16)

    q_off, k_off, v_off = 0, d_attn, d_attn + d_kv
    ga_off = v_off + d_kv
    u_off, vs_off, gb_off = ga_off + d_attn, ga_off + d_attn + d_gmlp, ga_off + d_attn + 2 * d_gmlp

    q = _head_rmsnorm(proj(q_off, d_attn), qn_ref[...])
    put(lay.q, q * (HEAD_DIM ** -0.5 * LOG2_E))
    k = _head_rmsnorm(proj(k_off, d_kv), kn_ref[...])
    for jb in range(k.shape[0] // WINDOW):
        for c in range(d_kv // LANES):
            tile = k[jb * WINDOW:(jb + 1) * WINDOW, c * LANES:(c + 1) * LANES]
            act_ref[jb * WINDOW:(jb + 1) * WINDOW,
                    lay.kt + c * LANES:lay.kt + (c + 1) * LANES] = tile.T.astype(BF16)
    vx, vy = _split_heads(proj(v_off, d_kv))
    put(lay.vx, vx)
    put(lay.vy, vy)
    put(lay.u, _gelu_exact(proj(u_off, d_gmlp)))
    put(lay.vs, _gelu_exact(proj(vs_off, d_gmlp)))
    put(lay.ga, _silu(proj(ga_off, d_attn)))
    put(lay.gb, _silu(proj(gb_off, d_gmlp)))


def _inproj(layer, x2d, norm_g, w_in_bf16, qn_tiled, kn_tiled, w_out, *, d_attn, d_kv, d_gmlp):
    t, d_model = x2d.shape
    d_in = w_in_bf16.shape[1]
    tm = TOKENS_PER_STEP
    width = _act_layout(d_attn, d_kv, d_gmlp).width
    row = lambda i: (i, 0)
    const = lambda i: (layer, 0, 0)
    cast_in, cast_out, cast_shape = _cast_slice_specs(w_out, layer, t // tm, lambda i: i)
    return pl.pallas_call(
        functools.partial(_inproj_kernel, d_attn=d_attn, d_kv=d_kv, d_gmlp=d_gmlp),
        grid=(t // tm,),
        in_specs=[
            pl.BlockSpec((tm, d_model), row),
            _resident((None, 1, d_model), const),
            _resident((d_model, d_in), lambda i: (0, 0)),
            _resident((None, 1, d_attn), const),
            _resident((None, 1, d_kv), const),
            cast_in,
        ],
        out_specs=[pl.BlockSpec((tm, width), row), cast_out],
        out_shape=[jax.ShapeDtypeStruct((t, width), BF16), cast_shape],
        compiler_params=_compiler_params(1),
        name="inproj",
    )(x2d, norm_g, w_in_bf16, qn_tiled, kn_tiled, w_out)


def _mixout_kernel(sinks_ref, x_ref, act_ref, prev_ref, ws_ref, bfull_ref, wout_ref, *rest,
                   layer, n_q_heads, n_gmlp_heads, tb, d_attn, convert_next_w_in):
    if convert_next_w_in:
        win_slice_ref, o_ref, win_bf16_ref, bias, wcat, ones = rest
        win_bf16_ref[...] = win_slice_ref[...].astype(BF16)
    else:
        o_ref, bias, wcat, ones = rest
    n_blocks = tb // WINDOW
    d_kv = N_KV_HEADS * HEAD_DIM
    d_gmlp = n_gmlp_heads * HEAD_DIM
    lay = _act_layout(d_attn, d_kv, d_gmlp)
    first_tile = (pl.program_id(0) == 0) & (pl.program_id(1) == 0)

    @pl.when(first_tile)
    def _init_tables():
        qpos = lax.broadcasted_iota(jnp.int32, (WINDOW, 2 * WINDOW), 0) + WINDOW
        kpos = lax.broadcasted_iota(jnp.int32, (WINDOW, 2 * WINDOW), 1)
        dist = qpos - kpos
        in_window = (dist >= 0) & (dist < WINDOW)
        dist_f = dist.astype(F32)
        for j in range(n_q_heads):
            slope = np.float32(2.0 ** (-8.0 * (j + 1) / n_q_heads))
            alibi = (-slope * LOG2_E) * dist_f
            sink = sinks_ref[layer, j] * LOG2_E
            rows = slice((j // 2 % 2) * WINDOW, (j // 2 % 2 + 1) * WINDOW)
            lanes = slice((j % 2) * 2 * WINDOW, (j % 2 + 1) * 2 * WINDOW)
            bias[1, j // 4, rows, lanes] = jnp.where(
                kpos == 0, sink, jnp.where(in_window, alibi, -jnp.inf))
            bias[0, j // 4, rows, lanes] = jnp.where(
                kpos == 0, sink, jnp.where(in_window & (kpos >= WINDOW), alibi, -jnp.inf))
        row = lax.broadcasted_iota(jnp.int32, (CHUNK, CHUNK), 0)
        col = lax.broadcasted_iota(jnp.int32, (CHUNK, CHUNK), 1)
        for hh in range(n_gmlp_heads):
            wcat[hh // 2, :, (hh % 2) * CHUNK:(hh % 2 + 1) * CHUNK] = (
                jnp.where(row >= col, ws_ref[hh], 0.0).astype(BF16))
        r = lax.broadcasted_iota(jnp.int32, (4 * WINDOW, LANES), 0)
        c = lax.broadcasted_iota(jnp.int32, (4 * WINDOW, LANES), 1)
        ones[...] = jnp.where((r < 2 * WINDOW) == (c < HEAD_DIM), 1.0, 0.0).astype(BF16)

    low_half = _low_half(WINDOW)
    has_prev0 = jnp.where(pl.program_id(1) == 0, 0, 1)
    zeros_kt = jnp.zeros((HEAD_DIM, 2 * WINDOW), BF16)

    def band(jb, rows_in_block, lanes, sink_slot_axis):
        prev = (prev_ref[rows_in_block, lanes] if jb == 0 else
                act_ref[(jb - 1) * WINDOW + rows_in_block.start:
                        (jb - 1) * WINDOW + rows_in_block.stop, lanes])
        position = lax.broadcasted_iota(jnp.int32, prev.shape, sink_slot_axis)
        prev = jnp.where(position == 0, jnp.zeros_like(prev), prev)
        own = act_ref[jb * WINDOW + rows_in_block.start:jb * WINDOW + rows_in_block.stop, lanes]
        return [prev, own]

    all_rows = slice(0, WINDOW)
    mix_vals = {}

    def block_rows(jb):
        return slice(jb * WINDOW, (jb + 1) * WINDOW)

    def q_cols(hk):
        return [slice((2 * hk + pr) * LANES, (2 * hk + pr + 1) * LANES) for pr in range(2)]

    def scores(jb, hk):
        feat_rows = slice((hk % 2) * HEAD_DIM, (hk % 2 + 1) * HEAD_DIM)
        kt_lanes = slice(lay.kt + (hk // 2) * LANES, lay.kt + (hk // 2 + 1) * LANES)
        kband = jnp.concatenate(band(jb, feat_rows, kt_lanes, 1), axis=1)
        krhs = jnp.concatenate([jnp.concatenate([kband, zeros_kt], axis=1),
                                jnp.concatenate([zeros_kt, kband], axis=1)], axis=0)
        q2 = jnp.concatenate([act_ref[block_rows(jb), lay.q + cl.start:lay.q + cl.stop]
                              for cl in q_cols(hk)], axis=0)
        s = jnp.dot(q2, krhs, preferred_element_type=F32)
        return s + (bias[has_prev0, hk] if jb == 0 else bias[1, hk])

    def softmax_numerators(s):
        probs = []
        for pr in range(2):
            s_a = s[pr * WINDOW:(pr + 1) * WINDOW, :2 * WINDOW]
            s_b = s[pr * WINDOW:(pr + 1) * WINDOW, 2 * WINDOW:]
            m_a = jnp.max(s_a, axis=-1, keepdims=True)
            m_b = jnp.max(s_b, axis=-1, keepdims=True)
            probs.append(jnp.concatenate([jnp.exp2(s_a - m_a), jnp.exp2(s_b - m_b)],
                                         axis=1).astype(BF16))
        return jnp.concatenate(probs, axis=0)

    def weighted_values(jb, hk, probs):
        vx_lanes = slice(lay.vx + hk * LANES, lay.vx + (hk + 1) * LANES)
        vy_lanes = slice(lay.vy + hk * LANES, lay.vy + (hk + 1) * LANES)
        vstack = jnp.concatenate(band(jb, all_rows, vx_lanes, 0) + band(jb, all_rows, vy_lanes, 0),
                                 axis=0)
        vrhs = jnp.concatenate([vstack, ones[...]], axis=1)
        return jnp.dot(probs, vrhs, preferred_element_type=F32)

    def normalize_and_gate(jb, hk, o):
        rows = block_rows(jb)
        for pr, cl in enumerate(q_cols(hk)):
            o_pr = o[pr * WINDOW:(pr + 1) * WINDOW, :]
            attn = o_pr[:, :LANES] / o_pr[:, LANES:]
            ga = act_ref[rows, lay.ga + cl.start:lay.ga + cl.stop]
            mix_vals[(jb, cl.start // LANES)] = attn.astype(BF16) * ga

    def spatial_gating_pair(blocks, jp):
        cl = slice(jp * LANES, (jp + 1) * LANES)
        tiles = []
        for jb in blocks:
            vpair = act_ref[block_rows(jb), lay.vs + cl.start:lay.vs + cl.stop]
            zero = jnp.zeros_like(vpair)
            tiles.append(jnp.concatenate([jnp.where(low_half, vpair, zero),
                                          jnp.where(low_half, zero, vpair)], axis=0))
        mixed = jnp.dot(wcat[jp], jnp.concatenate(tiles, axis=1),
                        preferred_element_type=F32)
        for n, jb in enumerate(blocks):
            rows = block_rows(jb)
            mixed_jb = mixed[:, n * LANES:(n + 1) * LANES] + bfull_ref[:, cl]
            u = act_ref[rows, lay.u + cl.start:lay.u + cl.stop]
            gb = act_ref[rows, lay.gb + cl.start:lay.gb + cl.stop]
            mix_vals[(jb, (d_attn + cl.start) // LANES)] = mixed_jb.astype(BF16) * (u * gb)

    def out_rows(jo):
        rows = slice(jo * OUT_ROWS, (jo + 1) * OUT_ROWS)
        group = range(jo * OUT_ROWS // WINDOW, (jo + 1) * OUT_ROWS // WINDOW)
        mix_value = jnp.concatenate(
            [jnp.concatenate([mix_vals[(jb, c)] for jb in group], axis=0)
             for c in range(wout_ref.shape[0] // LANES)], axis=1)
        o_ref[rows, :] = x_ref[rows, :] + jnp.dot(mix_value, wout_ref[...],
                                                  preferred_element_type=F32)

    blocks_per_out = OUT_ROWS // WINDOW
    for jo in range(n_blocks // blocks_per_out):
        blocks = range(jo * blocks_per_out, (jo + 1) * blocks_per_out)
        items = [(jb, hk) for jb in blocks for hk in range(N_KV_HEADS)]
        gating_pairs = list(range(n_gmlp_heads // 2))
        s_queue = [scores(*item) for item in items[:SCORES_AHEAD]]
        pending = None
        for idx, (jb, hk) in enumerate(items):
            if idx + SCORES_AHEAD < len(items):
                s_queue.append(scores(*items[idx + SCORES_AHEAD]))
            o = weighted_values(jb, hk, softmax_numerators(s_queue.pop(0)))
            if pending is not None:
                normalize_and_gate(*pending)
            pending = (jb, hk, o)
            if idx >= len(items) - len(gating_pairs) and gating_pairs:
                spatial_gating_pair(blocks, gating_pairs.pop(0))
        normalize_and_gate(*pending)
        for jp in gating_pairs:
            spatial_gating_pair(blocks, jp)
        out_rows(jo)


def _mixout(layer, sinks, x2d, act, w_s, b_full, w_out_bf16, w_in, *, batch, seq, d_attn, d_gmlp):
    t, d_model = x2d.shape
    convert_next_w_in = layer + 1 < w_in.shape[0]
    d_kv = N_KV_HEADS * HEAD_DIM
    lay = _act_layout(d_attn, d_kv, d_gmlp)
    n_q_heads = d_attn // HEAD_DIM
    n_gmlp_heads = d_gmlp // HEAD_DIM
    tb = TOKENS_PER_STEP
    tiles = seq // tb
    blocks_per_tile = tb // WINDOW
    row = lambda b, i: (b * tiles + i, 0)
    prev = lambda b, i: (b * (seq // WINDOW) + jnp.maximum(i * blocks_per_tile - 1, 0), 0)
    in_specs = [
        pl.BlockSpec(memory_space=pltpu.SMEM),
        pl.BlockSpec((tb, d_model), row),
        pl.BlockSpec((tb, lay.width), row),
        pl.BlockSpec((WINDOW, lay.band_width), prev),
        _resident((None, n_gmlp_heads, CHUNK, CHUNK), lambda b, i: (layer, 0, 0, 0)),
        _resident((None, CHUNK, d_gmlp), lambda b, i: (layer, 0, 0)),
        _resident((d_attn + d_gmlp, d_model), lambda b, i: (0, 0)),
    ]
    operands = [sinks, x2d, act, act, w_s, b_full, w_out_bf16]
    out_specs = [pl.BlockSpec((tb, d_model), row)]
    out_shape = [jax.ShapeDtypeStruct((t, d_model), F32)]
    if convert_next_w_in:
        cast_in, cast_out, cast_shape = _cast_slice_specs(
            w_in, layer + 1, batch * tiles, lambda b, i: b * tiles + i)
        in_specs.append(cast_in)
        operands.append(w_in)
        out_specs.append(cast_out)
        out_shape.append(cast_shape)
    outs = pl.pallas_call(
        functools.partial(_mixout_kernel, layer=layer, n_q_heads=n_q_heads,
                          n_gmlp_heads=n_gmlp_heads, tb=tb, d_attn=d_attn,
                          convert_next_w_in=convert_next_w_in),
        grid=(batch, tiles),
        in_specs=in_specs,
        out_specs=out_specs,
        out_shape=out_shape,
        scratch_shapes=[
            pltpu.VMEM((2, N_KV_HEADS, 2 * WINDOW, 4 * WINDOW), F32),
            pltpu.VMEM((n_gmlp_heads // 2, CHUNK, 2 * CHUNK), BF16),
            pltpu.VMEM((4 * WINDOW, LANES), BF16),
        ],
        compiler_params=_compiler_params(2),
        name="mixout",
    )(*operands)
    return outs if convert_next_w_in else (outs[0], None)


def kernel(x, norm_g, w_in, q_norm, k_norm, sinks, w_s, b_s, w_out):
    batch, seq, d_model = x.shape
    depth = norm_g.shape[0]
    n_q_heads = sinks.shape[1]
    n_gmlp_heads = w_s.shape[1]
    d_attn = n_q_heads * HEAD_DIM
    d_gmlp = n_gmlp_heads * HEAD_DIM
    d_kv = N_KV_HEADS * HEAD_DIM
    assert w_in.shape[2] == 2 * d_attn + 2 * d_kv + 3 * d_gmlp
    assert seq % TOKENS_PER_STEP == 0 and q_norm.shape[1] == HEAD_DIM
    assert n_q_heads == 4 * N_KV_HEADS and TOKENS_PER_STEP % OUT_ROWS == 0

    norm_g3 = norm_g.reshape(depth, 1, d_model)
    qn_tiled = jnp.tile(q_norm, (1, n_q_heads)).reshape(depth, 1, d_attn)
    kn_tiled = jnp.tile(k_norm, (1, N_KV_HEADS)).reshape(depth, 1, d_kv)
    b_full = jnp.repeat(jnp.swapaxes(b_s, 1, 2), HEAD_DIM, axis=2)

    x2d = x.reshape(batch * seq, d_model)
    w_in_bf16 = w_in[0].astype(BF16)
    for l in range(depth):
        act, w_out_bf16 = _inproj(l, x2d, norm_g3, w_in_bf16, qn_tiled, kn_tiled, w_out,
                                  d_attn=d_attn, d_kv=d_kv, d_gmlp=d_gmlp)
        x2d, w_in_bf16 = _mixout(l, sinks, x2d, act, w_s, b_full, w_out_bf16, w_in,
                                 batch=batch, seq=seq, d_attn=d_attn, d_gmlp=d_gmlp)
    return x2d.reshape(batch, seq, d_model)
```
